```python
import jax, jax.numpy as jnp
from jax import lax
import numpy as np

D_MODEL = 2048
BATCH = 4
SEQ = 2048
DEPTH = 2

EPS = 1e-6
NEG = -1e30
HEAD_DIM = 128
ROT_DIM = HEAD_DIM // 4
ROPE_THETA = 500000.0
DILATED_PATTERNS = ((128, 1), (512, 4), (2048, 16))
ATT_GROUPS = len(DILATED_PATTERNS)
HEADS_PER_GROUP = D_MODEL // (2 * HEAD_DIM)
ATT_QKV = ATT_GROUPS * HEADS_PER_GROUP * HEAD_DIM
ATT_OUT = HEADS_PER_GROUP * HEAD_DIM
POOL_WINDOWS = (2, 4, 8, 16)
POOL_GROUPS = len(POOL_WINDOWS)
POOL_WIDTH = D_MODEL // 2
POOL_CH = POOL_WIDTH // POOL_GROUPS
SGU_WIDTH = D_MODEL // 2
SGU_GROUPS = 4
SGU_CH = SGU_WIDTH // SGU_GROUPS
CHUNK = 128
CONV_WIDTH = D_MODEL // 2
CONV_K = 31
EVEN_COLS = (POOL_WIDTH, POOL_WIDTH, ATT_QKV, ATT_QKV, ATT_QKV, ATT_OUT)
ODD_COLS = (SGU_WIDTH, SGU_WIDTH, SGU_WIDTH, CONV_WIDTH, CONV_WIDTH, CONV_WIDTH)
EVEN_IN = sum(EVEN_COLS)
ODD_IN = sum(ODD_COLS)
MIX_OUT = POOL_WIDTH + ATT_OUT
N_EVEN = (DEPTH + 1) // 2
N_ODD = DEPTH // 2

kernel_name = "hybrid_pool_dilattn_sgu_conv"


def _split_points(cols):
    return [int(c) for c in np.cumsum(cols)[:-1]]


def rmsnorm(x, g):
    xf = x.astype(jnp.float32)
    y = xf * lax.rsqrt(jnp.mean(xf * xf, axis=-1, keepdims=True) + EPS) * g.astype(jnp.float32)
    return y.astype(x.dtype)


def layernorm(x, g, b):
    xf = x.astype(jnp.float32)
    mu = jnp.mean(xf, axis=-1, keepdims=True)
    var = jnp.mean(jnp.square(xf - mu), axis=-1, keepdims=True)
    y = (xf - mu) * lax.rsqrt(var + EPS) * g.astype(jnp.float32) + b.astype(jnp.float32)
    return y.astype(x.dtype)


def partial_rope(t, cos, sin):
    tf = t.astype(jnp.float32)
    half = ROT_DIM // 2
    t1, t2 = tf[..., :half], tf[..., half:ROT_DIM]
    c, s = cos[None, :, None, :], sin[None, :, None, :]
    out = jnp.concatenate([t1 * c - t2 * s, t2 * c + t1 * s, tf[..., ROT_DIM:]], axis=-1)
    return out.astype(t.dtype)


def causal_pool_mixer(xa, pool_w, pool_scale):
    B, S, _ = xa.shape
    xg = xa.reshape(B, S, POOL_GROUPS, POOL_CH).astype(jnp.float32)
    csp = jnp.concatenate([jnp.zeros((B, 1, POOL_GROUPS, POOL_CH), jnp.float32),
                           jnp.cumsum(xg, axis=1)], axis=1)
    t = jnp.arange(S)
    outs = []
    for g, w in enumerate(POOL_WINDOWS):
        upper = csp[:, 1:, g]
        lower = jnp.concatenate([jnp.zeros((B, w - 1, POOL_CH), jnp.float32),
                                 csp[:, :S + 1 - w, g]], axis=1)
        count = jnp.minimum(t + 1, w).astype(jnp.float32)[None, :, None]
        outs.append((upper - lower) / count - xg[:, :, g])
    pooled = jnp.stack(outs, axis=2).astype(xa.dtype)
    mixed = jnp.einsum('bsgc,gcd->bsgd', pooled, pool_w)
    return mixed.reshape(B, S, POOL_WIDTH) * pool_scale


def dilated_group(q, k, v, dilation, span):
    B, S, H, E = q.shape
    L = S // dilation
    nb = -(-L // span)
    Lp = nb * span

    def to_blocks(t):
        t = t.reshape(B, L, dilation, H, E)
        t = jnp.pad(t, ((0, 0), (0, Lp - L), (0, 0), (0, 0), (0, 0)))
        return t.reshape(B, nb, span, dilation, H, E)

    def with_prev(t):
        prev = jnp.pad(t, ((0, 0), (1, 0), (0, 0), (0, 0), (0, 0), (0, 0)))[:, :-1]
        return jnp.concatenate([prev, t], axis=2)

    qb = to_blocks(q)
    kk = with_prev(to_blocks(k))
    vv = with_prev(to_blocks(v))
    s = jnp.einsum('bnqrhe,bnkrhe->bnrhqk', qb, kk,
                   preferred_element_type=jnp.float32) * (HEAD_DIM ** -0.5)
    qi = jnp.arange(span)[:, None]
    kj = jnp.arange(2 * span)[None, :] - span
    dist = qi - kj
    blk = jnp.arange(nb)[:, None, None]
    valid = (dist >= 0)[None] & (dist <= span)[None] & (blk * span + kj[None] >= 0)
    s = jnp.where(valid[None, :, None, None], s, NEG)
    m = jnp.max(s, axis=-1, keepdims=True)
    p = jnp.exp(s - m)
    den = jnp.sum(p, axis=-1)
    o = jnp.einsum('bnrhqk,bnkrhe->bnqrhe', p.astype(vv.dtype), vv,
                   preferred_element_type=jnp.float32)
    den_t = jnp.transpose(den, (0, 1, 4, 2, 3))
    o = o / den_t[..., None]
    lse = jnp.transpose(m[..., 0], (0, 1, 4, 2, 3)) + jnp.log(den_t)
    o = o.reshape(B, Lp, dilation, H, E)[:, :L].reshape(B, S, H, E)
    lse = lse.reshape(B, Lp, dilation, H)[:, :L].reshape(B, S, H)
    return o, lse


def dilated_attention(q, k, v, cos, sin):
    B, S, _ = q.shape
    shp = (B, S, ATT_GROUPS * HEADS_PER_GROUP, HEAD_DIM)
    q = partial_rope(q.reshape(shp), cos, sin).reshape(B, S, ATT_GROUPS, HEADS_PER_GROUP, HEAD_DIM)
    k = partial_rope(k.reshape(shp), cos, sin).reshape(B, S, ATT_GROUPS, HEADS_PER_GROUP, HEAD_DIM)
    v = v.reshape(B, S, ATT_GROUPS, HEADS_PER_GROUP, HEAD_DIM)
    outs, lses = [], []
    for g, (window, dilation) in enumerate(DILATED_PATTERNS):
        o_g, lse_g = dilated_group(q[:, :, g], k[:, :, g], v[:, :, g], dilation, window // dilation)
        outs.append(o_g)
        lses.append(lse_g)
    wts = jax.nn.softmax(jnp.stack(lses, axis=0), axis=0)
    o = jnp.sum(wts[..., None] * jnp.stack(outs, axis=0), axis=0)
    return o.reshape(B, S, ATT_OUT).astype(q.dtype)


def chunked_sgu(u, v, g, b, w_s, b_s):
    B, S, _ = v.shape
    vn = layernorm(v, g, b).reshape(B, S // CHUNK, CHUNK, SGU_GROUPS, SGU_CH)
    mask = jnp.tril(jnp.ones((CHUNK, CHUNK), w_s.dtype))
    s = jnp.einsum('hij,bnjhc->bnihc', w_s * mask[None], vn) + b_s.T[None, None, :, :, None]
    return u * s.reshape(B, S, SGU_WIDTH)


def causal_depthwise_conv(x, w, b):
    out = lax.conv_general_dilated(x, w[:, None, :].astype(x.dtype), window_strides=(1,),
                                   padding=[(CONV_K - 1, 0)],
                                   dimension_numbers=('NWC', 'WIO', 'NWC'),
                                   feature_group_count=x.shape[-1])
    return out + b


def even_mixer(h, w_in, pool_w, pool_scale, w_out, cos, sin):
    z = h @ w_in
    a_in, a_gate, q, k, v, b_gate = jnp.split(z, _split_points(EVEN_COLS), axis=-1)
    ya = causal_pool_mixer(a_in, pool_w, pool_scale) * jax.nn.silu(a_gate)
    yb = dilated_attention(q, k, v, cos, sin) * jax.nn.silu(b_gate)
    return jnp.concatenate([ya, yb], axis=-1) @ w_out


def odd_mixer(h, w_in, sgu_g, sgu_b, sgu_w, sgu_bias, conv_w, conv_b, cn_g, cn_b, w_out):
    z = h @ w_in
    u, v, c_gate, d_val, d_glu, d_gate = jnp.split(z, _split_points(ODD_COLS), axis=-1)
    yc = chunked_sgu(u, v, sgu_g, sgu_b, sgu_w, sgu_bias) * jax.nn.silu(c_gate)
    d = d_val * jax.nn.sigmoid(d_glu)
    d = causal_depthwise_conv(d, conv_w, conv_b)
    d = jax.nn.silu(layernorm(d, cn_g, cn_b))
    yd = d * jax.nn.silu(d_gate)
    return jnp.concatenate([yc, yd], axis=-1) @ w_out


def setup_inputs(seed: int = 0) -> dict:
    key = jax.random.key(seed)
    ks = jax.random.split(key, 20)
    f32 = jnp.float32

    def nrm(k, shape, scale):
        return jax.random.normal(k, shape, f32) * scale

    def gain(k, shape):
        return 1.0 + 0.05 * jax.random.normal(k, shape, f32)

    return {
        "x": jax.random.normal(ks[0], (BATCH, SEQ, D_MODEL), f32),
        "e_pre_norm": gain(ks[1], (N_EVEN, D_MODEL)),
        "e_w_in": nrm(ks[2], (N_EVEN, D_MODEL, EVEN_IN), D_MODEL ** -0.5),
        "e_pool_w": nrm(ks[3], (N_EVEN, POOL_GROUPS, POOL_CH, POOL_CH), POOL_CH ** -0.5),
        "e_pool_scale": gain(ks[4], (N_EVEN, POOL_WIDTH)),
        "e_w_out": nrm(ks[5], (N_EVEN, MIX_OUT, D_MODEL), MIX_OUT ** -0.5),
        "e_post_norm": gain(ks[6], (N_EVEN, D_MODEL)),
        "o_pre_norm": gain(ks[7], (N_ODD, D_MODEL)),
        "o_w_in": nrm(ks[8], (N_ODD, D_MODEL, ODD_IN), D_MODEL ** -0.5),
        "o_sgu_norm_g": gain(ks[9], (N_ODD, SGU_WIDTH)),
        "o_sgu_norm_b": nrm(ks[10], (N_ODD, SGU_WIDTH), 0.02),
        "o_sgu_w": nrm(ks[11], (N_ODD, SGU_GROUPS, CHUNK, CHUNK), CHUNK ** -0.5),
        "o_sgu_b": gain(ks[12], (N_ODD, SGU_GROUPS, CHUNK)),
        "o_conv_w": nrm(ks[13], (N_ODD, CONV_K, CONV_WIDTH), CONV_K ** -0.5),
        "o_conv_b": nrm(ks[14], (N_ODD, CONV_WIDTH), 0.02),
        "o_conv_norm_g": gain(ks[15], (N_ODD, CONV_WIDTH)),
        "o_conv_norm_b": nrm(ks[16], (N_ODD, CONV_WIDTH), 0.02),
        "o_w_out": nrm(ks[17], (N_ODD, MIX_OUT, D_MODEL), MIX_OUT ** -0.5),
        "o_post_norm": gain(ks[18], (N_ODD, D_MODEL)),
    }


def reference(x, e_pre_norm, e_w_in, e_pool_w, e_pool_scale, e_w_out, e_post_norm,
              o_pre_norm, o_w_in, o_sgu_norm_g, o_sgu_norm_b, o_sgu_w, o_sgu_b,
              o_conv_w, o_conv_b, o_conv_norm_g, o_conv_norm_b, o_w_out, o_post_norm):
    S = x.shape[1]
    pos = jnp.arange(S, dtype=jnp.float32)
    inv_freq = jnp.power(ROPE_THETA, -jnp.arange(0, ROT_DIM, 2, dtype=jnp.float32) / ROT_DIM)
    ang = pos[:, None] * inv_freq[None, :]
    cos, sin = jnp.cos(ang), jnp.sin(ang)
    for i in range(DEPTH):
        j = i // 2
        if i % 2 == 0:
            h = rmsnorm(x, e_pre_norm[j])
            y = even_mixer(h, e_w_in[j], e_pool_w[j], e_pool_scale[j], e_w_out[j], cos, sin)
            x = x + rmsnorm(y, e_post_norm[j])
        else:
            h = rmsnorm(x, o_pre_norm[j])
            y = odd_mixer(h, o_w_in[j], o_sgu_norm_g[j], o_sgu_norm_b[j], o_sgu_w[j], o_sgu_b[j],
                          o_conv_w[j], o_conv_b[j], o_conv_norm_g[j], o_conv_norm_b[j], o_w_out[j])
            x = x + rmsnorm(y, o_post_norm[j])
    return x
```

```python
import functools

import jax
import jax.numpy as jnp
import numpy as np
from jax import lax
from jax.experimental import pallas as pl
from jax.experimental.pallas import tpu as pltpu

F32 = jnp.float32
BF16 = jnp.bfloat16

EPS = 1e-6
NEG = -1e30
HEAD_DIM = 128
ROT_DIM = HEAD_DIM // 4
ROT_HALF = ROT_DIM // 2
ROPE_THETA = 500000.0
DILATIONS = (1, 4, 16)
SPAN = 128
ATT_GROUPS = len(DILATIONS)
POOL_WINDOWS = (2, 4, 8, 16)
POOL_CH = 256
SGU_GROUPS = 4
SGU_CH = 256
CHUNK = 128
CONV_K = 31

LANES = 128
VMEM_LIMIT = 56 * 1024 * 1024


def _params(sem, vmem=VMEM_LIMIT):
    return pltpu.CompilerParams(dimension_semantics=sem, vmem_limit_bytes=vmem)


def _silu(x):
    return x * jax.nn.sigmoid(x)


def _rmsnorm_kernel(x_ref, g_ref, o_ref):
    x = x_ref[...]
    ms = jnp.mean(x * x, axis=-1, keepdims=True)
    o_ref[...] = (x * lax.rsqrt(ms + EPS) * g_ref[...]).astype(o_ref.dtype)


def _rmsnorm(x, g, tm=512):
    t, d = x.shape
    return pl.pallas_call(
        _rmsnorm_kernel,
        grid=(t // tm,),
        in_specs=[pl.BlockSpec((tm, d), lambda i: (i, 0)),
                  pl.BlockSpec((1, d), lambda i: (0, 0))],
        out_specs=pl.BlockSpec((tm, d), lambda i: (i, 0)),
        out_shape=jax.ShapeDtypeStruct((t, d), BF16),
        compiler_params=_params(("parallel",)),
        name="rmsnorm",
    )(x, g.reshape(1, d))


def _matmul_kernel(a_ref, w_ref, o_ref):
    o_ref[...] = jnp.dot(a_ref[...], w_ref[...],
                         preferred_element_type=F32).astype(o_ref.dtype)


def _matmul(a, w, tm=1024, tn=1024, name="in_proj"):
    m, k = a.shape
    _, n = w.shape
    return pl.pallas_call(
        _matmul_kernel,
        grid=(n // tn, m // tm),
        in_specs=[pl.BlockSpec((tm, k), lambda j, i: (i, 0)),
                  pl.BlockSpec((k, tn), lambda j, i: (0, j))],
        out_specs=pl.BlockSpec((tm, tn), lambda j, i: (i, j)),
        out_shape=jax.ShapeDtypeStruct((m, n), BF16),
        compiler_params=_params(("parallel", "parallel")),
        name=name,
    )(a, w)


POOL_PAD = 16
POOL_ROWS = 64


def _gapped(row_start, n_rows):
    return pl.ds(2 * row_start, n_rows, stride=2)


def _pool_kernel(z_ref, pw_ref, ps_ref, o_ref, xpad_ref):
    s_len = z_ref.shape[0]
    width = len(POOL_WINDOWS) * POOL_CH
    n_slabs = width // LANES
    slabs_per_group = POOL_CH // LANES
    for c in range(n_slabs):
        xpad_ref[c, 0:2 * POOL_PAD, :] = jnp.zeros((2 * POOL_PAD, LANES), F32)

    def fill(i, carry):
        r = pl.multiple_of(i * POOL_ROWS, POOL_ROWS)
        for c in range(n_slabs):
            xpad_ref[c, _gapped(POOL_PAD + r, POOL_ROWS), :] = (
                z_ref[pl.ds(r, POOL_ROWS), c * LANES:(c + 1) * LANES].astype(F32))
        return carry

    lax.fori_loop(0, s_len // POOL_ROWS, fill, 0)

    def body(i, carry):
        r = pl.multiple_of(i * POOL_ROWS, POOL_ROWS)
        row = r + lax.broadcasted_iota(jnp.int32, (POOL_ROWS, POOL_CH), 0)
        for g, w in enumerate(POOL_WINDOWS):
            cs = slice(g * POOL_CH, (g + 1) * POOL_CH)
            xs, accs = [], []
            for c in range(g * slabs_per_group, (g + 1) * slabs_per_group):
                xc = xpad_ref[c, _gapped(POOL_PAD + r, POOL_ROWS), :]
                ac = xc
                for j in range(1, w):
                    ac = ac + xpad_ref[c, _gapped(POOL_PAD + r - j, POOL_ROWS), :]
                xs.append(xc)
                accs.append(ac)
            x = jnp.concatenate(xs, axis=-1)
            acc = jnp.concatenate(accs, axis=-1)
            cnt = jnp.minimum(row + 1, w).astype(F32)
            pooled = acc / cnt - x
            mixed = jnp.dot(pooled.astype(BF16), pw_ref[g], preferred_element_type=F32)
            gate = z_ref[pl.ds(r, POOL_ROWS), width + g * POOL_CH: width + (g + 1) * POOL_CH].astype(F32)
            y = mixed * ps_ref[:, cs] * _silu(gate)
            o_ref[pl.ds(r, POOL_ROWS), cs] = y.astype(o_ref.dtype)
        return c

    lax.fori_loop(0, s_len // POOL_ROWS, body, 0)


def _pool_mixer(z3, pool_w, pool_scale):
    b, s, _ = z3.shape
    width = len(POOL_WINDOWS) * POOL_CH
    return pl.pallas_call(
        _pool_kernel,
        grid=(b,),
        in_specs=[pl.BlockSpec((None, s, 2 * width), lambda i: (i, 0, 0)),
                  pl.BlockSpec(pool_w.shape, lambda i: (0, 0, 0)),
                  pl.BlockSpec((1, width), lambda i: (0, 0))],
        out_specs=pl.BlockSpec((None, s, width), lambda i: (i, 0, 0)),
        out_shape=jax.ShapeDtypeStruct((b, s, width), BF16),
        scratch_shapes=[pltpu.VMEM((width // LANES, 2 * (POOL_PAD + s), LANES), F32)],
        compiler_params=_params(("parallel",)),
        name="pool_mixer",
    )(z3, pool_w, pool_scale.reshape(1, width))


ATT_ROWS = 256


def _attn_block(q, k, v, mask):
    s = lax.dot_general(q, k, (((1,), (1,)), ((), ())), preferred_element_type=F32)
    s = jnp.where(mask, s * (HEAD_DIM ** -0.5), NEG)
    m = jnp.max(s, axis=-1, keepdims=True)
    p = jnp.exp(s - m)
    den = jnp.sum(p, axis=-1, keepdims=True)
    o = jnp.dot(p.astype(BF16), v, preferred_element_type=F32)
    return o / den, m + jnp.log(den)


def _attn_kernel(q0, k0, v0, q1, k1, v1, q2, k2, v2, gate_ref, cos_ref, sa_ref, sb_ref,
                 o_ref, qkv_ref, og_ref, lse_ref):
    s_len = o_ref.shape[0]
    qkv_in = ((q0, k0, v0), (q1, k1, v1), (q2, k2, v2))

    def prep(i, c):
        r = pl.multiple_of(i * ATT_ROWS, ATT_ROWS)
        rows = pl.ds(r, ATT_ROWS)
        cos, sa, sb = cos_ref[rows, :], sa_ref[rows, :], sb_ref[rows, :]
        for g in range(ATT_GROUPS):
            for j in range(2):
                t = qkv_in[g][j][rows, :].astype(F32)
                up = pltpu.roll(t, HEAD_DIM - ROT_HALF, 1)
                down = pltpu.roll(t, ROT_HALF, 1)
                qkv_ref[3 * g + j, rows, :] = t * cos + up * sa + down * sb
            qkv_ref[3 * g + 2, rows, :] = qkv_in[g][2][rows, :].astype(F32)
        return c

    lax.fori_loop(0, s_len // ATT_ROWS, prep, 0)

    qi = lax.broadcasted_iota(jnp.int32, (SPAN, 2 * SPAN), 0)
    kj = lax.broadcasted_iota(jnp.int32, (SPAN, 2 * SPAN), 1)
    band = (kj >= qi) & (kj <= qi + SPAN)
    causal = (lax.broadcasted_iota(jnp.int32, (SPAN, SPAN), 1)
              <= lax.broadcasted_iota(jnp.int32, (SPAN, SPAN), 0))

    def run_block(g, d, q_start, k_start, n_keys, mask):
        def rows(start, n):
            return pl.ds(start, n) if d == 1 else pl.ds(start, n, stride=d)
        q = qkv_ref[3 * g, rows(q_start, SPAN), :].astype(BF16)
        k = qkv_ref[3 * g + 1, rows(k_start, n_keys), :].astype(BF16)
        v = qkv_ref[3 * g + 2, rows(k_start, n_keys), :].astype(BF16)
        o, lse = _attn_block(q, k, v, mask)
        og_ref[g, rows(q_start, SPAN), :] = o
        lse_ref[g, rows(q_start, SPAN), :] = jnp.broadcast_to(lse, (SPAN, HEAD_DIM))

    for g, d in enumerate(DILATIONS):
        n_blocks = s_len // (d * SPAN)
        stride_blk = d * SPAN

        def per_residue(r, c, g=g, d=d, n_blocks=n_blocks, stride_blk=stride_blk):
            run_block(g, d, r, r, SPAN, causal)

            def per_block(i, c2):
                q_start = r + i * stride_blk
                k_start = q_start - stride_blk
                if d == 1:
                    q_start = pl.multiple_of(q_start, SPAN)
                    k_start = pl.multiple_of(k_start, SPAN)
                run_block(g, d, q_start, k_start, 2 * SPAN, band)
                return c2

            if n_blocks > 1:
                lax.fori_loop(1, n_blocks, per_block, 0)
            return c

        if d == 1:
            per_residue(0, 0)
        else:
            lax.fori_loop(0, d, per_residue, 0)

    def merge(i, c):
        r = pl.multiple_of(i * ATT_ROWS, ATT_ROWS)
        rows = pl.ds(r, ATT_ROWS)
        l0, l1, l2 = lse_ref[0, rows, :], lse_ref[1, rows, :], lse_ref[2, rows, :]
        m = jnp.maximum(jnp.maximum(l0, l1), l2)
        e0, e1, e2 = jnp.exp(l0 - m), jnp.exp(l1 - m), jnp.exp(l2 - m)
        num = e0 * og_ref[0, rows, :] + e1 * og_ref[1, rows, :] + e2 * og_ref[2, rows, :]
        o = num / (e0 + e1 + e2)
        gate = gate_ref[rows, :].astype(F32)
        o_ref[rows, :] = (o * _silu(gate)).astype(o_ref.dtype)
        return c

    lax.fori_loop(0, s_len // ATT_ROWS, merge, 0)


def _rope_tables(s_len):
    pos = jnp.arange(s_len, dtype=F32)
    inv_freq = jnp.power(ROPE_THETA, -jnp.arange(0, ROT_DIM, 2, dtype=F32) / ROT_DIM)
    ang = pos[:, None] * inv_freq[None, :]
    cos, sin = jnp.cos(ang), jnp.sin(ang)
    zeros = jnp.zeros((s_len, HEAD_DIM - ROT_DIM), F32)
    zh = jnp.zeros((s_len, ROT_HALF), F32)
    cos_t = jnp.concatenate([cos, cos, jnp.ones((s_len, HEAD_DIM - ROT_DIM), F32)], axis=-1)
    sa_t = jnp.concatenate([-sin, zh, zeros], axis=-1)
    sb_t = jnp.concatenate([zh, sin, zeros], axis=-1)
    return cos_t, sa_t, sb_t


def _dilated_attention(z3, q_col, n_heads):
    b, s, _ = z3.shape
    per = ATT_GROUPS * n_heads
    qb = q_col // HEAD_DIM
    cos_t, sa_t, sb_t = _rope_tables(s)

    def head_spec(blk0):
        return pl.BlockSpec((None, s, HEAD_DIM), lambda i, h, blk0=blk0: (i, 0, blk0 + h))

    in_specs = []
    for g in range(ATT_GROUPS):
        for j in range(3):
            in_specs.append(head_spec(qb + j * per + g * n_heads))
    in_specs.append(head_spec(qb + 3 * per))
    tab_spec = pl.BlockSpec((s, HEAD_DIM), lambda i, h: (0, 0))
    in_specs += [tab_spec, tab_spec, tab_spec]
    return pl.pallas_call(
        _attn_kernel,
        grid=(b, n_heads),
        in_specs=in_specs,
        out_specs=pl.BlockSpec((None, s, HEAD_DIM), lambda i, h: (i, 0, h)),
        out_shape=jax.ShapeDtypeStruct((b, s, n_heads * HEAD_DIM), BF16),
        scratch_shapes=[pltpu.VMEM((3 * ATT_GROUPS, s, HEAD_DIM), F32),
                        pltpu.VMEM((ATT_GROUPS, s, HEAD_DIM), F32),
                        pltpu.VMEM((ATT_GROUPS, s, HEAD_DIM), F32)],
        compiler_params=_params(("parallel", "parallel")),
        name="dilated_attention",
    )(*([z3] * 10), cos_t, sa_t, sb_t)


def _out_proj_kernel(ya_ref, yb_ref, w_ref, x_ref, gpost_ref, *rest, with_next):
    half = ya_ref.shape[1]
    y = jnp.dot(ya_ref[...], w_ref[0:half, :], preferred_element_type=F32)
    y = y + jnp.dot(yb_ref[...], w_ref[half:2 * half, :], preferred_element_type=F32)
    yn = y * lax.rsqrt(jnp.mean(y * y, axis=-1, keepdims=True) + EPS) * gpost_ref[...]
    xn = x_ref[...] + yn
    if with_next:
        gnext_ref, xo_ref, ho_ref = rest
        xo_ref[...] = xn
        hn = xn * lax.rsqrt(jnp.mean(xn * xn, axis=-1, keepdims=True) + EPS) * gnext_ref[...]
        ho_ref[...] = hn.astype(ho_ref.dtype)
    else:
        (xo_ref,) = rest
        xo_ref[...] = xn


def _out_proj(ya, yb, w, x, g_post, g_next=None, tm=256):
    t, half = ya.shape
    d = w.shape[1]
    with_next = g_next is not None
    row = lambda i: (i, 0)
    fixed = lambda i: (0, 0)
    in_specs = [pl.BlockSpec((tm, half), row), pl.BlockSpec((tm, half), row),
                pl.BlockSpec(w.shape, fixed), pl.BlockSpec((tm, d), row),
                pl.BlockSpec((1, d), fixed)]
    args = [ya, yb, w, x, g_post.reshape(1, d)]
    out_specs = [pl.BlockSpec((tm, d), row)]
    out_shape = [jax.ShapeDtypeStruct((t, d), F32)]
    if with_next:
        in_specs.append(pl.BlockSpec((1, d), fixed))
        args.append(g_next.reshape(1, d))
        out_specs.append(pl.BlockSpec((tm, d), row))
        out_shape.append(jax.ShapeDtypeStruct((t, d), BF16))
    return pl.pallas_call(
        functools.partial(_out_proj_kernel, with_next=with_next),
        grid=(t // tm,),
        in_specs=in_specs,
        out_specs=out_specs,
        out_shape=out_shape,
        compiler_params=_params(("parallel",)),
        name="out_proj",
    )(*args)


def _sgu_kernel(u_ref, v_ref, cg_ref, g_ref, b_ref, ws_ref, bias_ref, o_ref):
    tm = u_ref.shape[0]
    ii = lax.broadcasted_iota(jnp.int32, (CHUNK, CHUNK), 0)
    jj = lax.broadcasted_iota(jnp.int32, (CHUNK, CHUNK), 1)
    tril = jj <= ii
    for c in range(tm // CHUNK):
        rows = slice(c * CHUNK, (c + 1) * CHUNK)
        v = v_ref[rows, :].astype(F32)
        mu = jnp.mean(v, axis=-1, keepdims=True)
        vc = v - mu
        var = jnp.mean(vc * vc, axis=-1, keepdims=True)
        vn = (vc * lax.rsqrt(var + EPS) * g_ref[...] + b_ref[...]).astype(BF16)
        for h in range(SGU_GROUPS):
            cs = slice(h * SGU_CH, (h + 1) * SGU_CH)
            wm = jnp.where(tril, ws_ref[h], 0.0).astype(BF16)
            sg = jnp.dot(wm, vn[:, cs], preferred_element_type=F32) + bias_ref[:, cs]
            y = u_ref[rows, cs].astype(F32) * sg * _silu(cg_ref[rows, cs].astype(F32))
            o_ref[rows, cs] = y.astype(o_ref.dtype)


def _sgu(z, sgu_g, sgu_b, sgu_w, sgu_bias, tm=512):
    t = z.shape[0]
    width = SGU_GROUPS * SGU_CH
    bias_full = jnp.repeat(sgu_bias.T, SGU_CH, axis=1)
    fixed2 = lambda i: (0, 0)
    return pl.pallas_call(
        _sgu_kernel,
        grid=(t // tm,),
        in_specs=[pl.BlockSpec((tm, width), lambda i: (i, 0)),
                  pl.BlockSpec((tm, width), lambda i: (i, 1)),
                  pl.BlockSpec((tm, width), lambda i: (i, 2)),
                  pl.BlockSpec((1, width), fixed2), pl.BlockSpec((1, width), fixed2),
                  pl.BlockSpec(sgu_w.shape, lambda i: (0, 0, 0)),
                  pl.BlockSpec((CHUNK, width), fixed2)],
        out_specs=pl.BlockSpec((tm, width), lambda i: (i, 0)),
        out_shape=jax.ShapeDtypeStruct((t, width), BF16),
        compiler_params=_params(("parallel",)),
        name="sgu",
    )(z, z, z, sgu_g.reshape(1, width), sgu_b.reshape(1, width), sgu_w, bias_full)


CONV_PAD = 32
CONV_ROWS = 128
CONV_FILL = 128


def _conv_kernel(dv_ref, dg_ref, w_ref, cb_ref, o_ref, dpad_ref):
    s_len = dv_ref.shape[0]
    dpad_ref[0:2 * CONV_PAD, :] = jnp.zeros((2 * CONV_PAD, LANES), F32)

    def fill(i, carry):
        r = pl.multiple_of(i * CONV_FILL, CONV_FILL)
        rows = pl.ds(r, CONV_FILL)
        d = dv_ref[rows, :].astype(F32) * jax.nn.sigmoid(dg_ref[rows, :].astype(F32))
        dpad_ref[_gapped(CONV_PAD + r, CONV_FILL), :] = d
        return carry

    lax.fori_loop(0, s_len // CONV_FILL, fill, 0)

    lead = CONV_PAD - (CONV_K - 1)

    def body(i, carry):
        r = pl.multiple_of(i * CONV_ROWS, CONV_ROWS)
        acc = jnp.zeros((CONV_ROWS, LANES), F32)
        for k in range(CONV_K):
            acc = acc + dpad_ref[_gapped(r + lead + k, CONV_ROWS), :] * w_ref[k:k + 1, :]
        o_ref[pl.ds(r, CONV_ROWS), :] = acc + cb_ref[...]
        return carry

    lax.fori_loop(0, s_len // CONV_ROWS, body, 0)


def _conv_norm_kernel(c_ref, gate_ref, ng_ref, nb_ref, o_ref):
    acc = c_ref[...]
    mu = jnp.mean(acc, axis=-1, keepdims=True)
    ac = acc - mu
    var = jnp.mean(ac * ac, axis=-1, keepdims=True)
    y = _silu(ac * lax.rsqrt(var + EPS) * ng_ref[...] + nb_ref[...])
    o_ref[...] = (y * _silu(gate_ref[...].astype(F32))).astype(o_ref.dtype)


def _conv_module(z3, col, conv_w, conv_b, cn_g, cn_b, tm=512):
    b, s, cols = z3.shape
    ch = conv_w.shape[1]
    n_slabs = ch // LANES
    blk = col // LANES
    vec = lambda a: a.reshape(1, ch)
    conv = pl.pallas_call(
        _conv_kernel,
        grid=(b, n_slabs),
        in_specs=[pl.BlockSpec((None, s, LANES), lambda i, c: (i, 0, blk + c)),
                  pl.BlockSpec((None, s, LANES), lambda i, c: (i, 0, blk + n_slabs + c)),
                  pl.BlockSpec((CONV_K, LANES), lambda i, c: (0, c)),
                  pl.BlockSpec((1, LANES), lambda i, c: (0, c))],
        out_specs=pl.BlockSpec((None, s, LANES), lambda i, c: (i, 0, c)),
        out_shape=jax.ShapeDtypeStruct((b, s, ch), F32),
        scratch_shapes=[pltpu.VMEM((2 * (CONV_PAD + s), LANES), F32)],
        compiler_params=_params(("parallel", "parallel")),
        name="conv_taps",
    )(z3, z3, conv_w, vec(conv_b))
    t = b * s
    fixed2 = lambda i: (0, 0)
    gate_blk = col // ch + 2
    return pl.pallas_call(
        _conv_norm_kernel,
        grid=(t // tm,),
        in_specs=[pl.BlockSpec((tm, ch), lambda i: (i, 0)),
                  pl.BlockSpec((tm, ch), lambda i: (i, gate_blk)),
                  pl.BlockSpec((1, ch), fixed2), pl.BlockSpec((1, ch), fixed2)],
        out_specs=pl.BlockSpec((tm, ch), lambda i: (i, 0)),
        out_shape=jax.ShapeDtypeStruct((t, ch), BF16),
        compiler_params=_params(("parallel",)),
        name="conv_norm",
    )(conv.reshape(t, ch), z3.reshape(t, cols), vec(cn_g), vec(cn_b))


def kernel(x, e_pre_norm, e_w_in, e_pool_w, e_pool_scale, e_w_out, e_post_norm, o_pre_norm, o_w_in, o_sgu_norm_g, o_sgu_norm_b, o_sgu_w, o_sgu_b, o_conv_w, o_conv_b, o_conv_norm_g, o_conv_norm_b, o_w_out, o_post_norm):
    b, s, d = x.shape
    t = b * s
    n_even, n_odd = e_w_in.shape[0], o_w_in.shape[0]
    depth = n_even + n_odd
    pool_width = len(POOL_WINDOWS) * POOL_CH
    n_heads = (e_w_out.shape[1] - pool_width) // HEAD_DIM
    sgu_width = SGU_GROUPS * SGU_CH

    def pre_gain(i):
        return e_pre_norm[i // 2] if i % 2 == 0 else o_pre_norm[i // 2]

    xf = x.reshape(t, d)
    h = _rmsnorm(xf, pre_gain(0))
    for i in range(depth):
        j = i // 2
        g_next = pre_gain(i + 1) if i + 1 < depth else None
        if i % 2 == 0:
            z = _matmul(h, e_w_in[j].astype(BF16))
            z3 = z.reshape(b, s, z.shape[1])
            ya = _pool_mixer(z3, e_pool_w[j].astype(BF16), e_pool_scale[j])
            yb = _dilated_attention(z3, 2 * pool_width, n_heads)
            w_out, g_post = e_w_out[j], e_post_norm[j]
        else:
            z = _matmul(h, o_w_in[j].astype(BF16))
            z3 = z.reshape(b, s, z.shape[1])
            ya = _sgu(z, o_sgu_norm_g[j], o_sgu_norm_b[j], o_sgu_w[j], o_sgu_b[j]).reshape(b, s, sgu_width)
            yb = _conv_module(z3, 3 * sgu_width, o_conv_w[j], o_conv_b[j],
                              o_conv_norm_g[j], o_conv_norm_b[j])
            w_out, g_post = o_w_out[j], o_post_norm[j]
        outs = _out_proj(ya.reshape(t, -1), yb.reshape(t, -1), w_out.astype(BF16), xf, g_post, g_next)
        if g_next is None:
            xf = outs[0]
        else:
            xf, h = outs
    return xf.reshape(b, s, d)
```

```python
import functools

import jax
import jax.numpy as jnp
import numpy as np
from jax import lax
from jax.experimental import pallas as pl
from jax.experimental.pallas import tpu as pltpu

F32 = jnp.float32
BF16 = jnp.bfloat16

EPS = 1e-6
NEG = -1e30
HEAD_DIM = 128
ROT_DIM = HEAD_DIM // 4
ROT_HALF = ROT_DIM // 2
ROPE_THETA = 500000.0
DILATIONS = (1, 4, 16)
SPAN = 128
ATT_GROUPS = len(DILATIONS)
POOL_WINDOWS = (2, 4, 8, 16)
POOL_CH = 256
SGU_GROUPS = 4
SGU_CH = 256
CHUNK = 128
CONV_K = 31

LANES = 128
VMEM_LIMIT = 56 * 1024 * 1024


def _params(sem, vmem=VMEM_LIMIT):
    return pltpu.CompilerParams(dimension_semantics=sem, vmem_limit_bytes=vmem)


def _silu(x):
    return x * jax.nn.sigmoid(x)


def _rmsnorm_kernel(x_ref, g_ref, o_ref):
    x = x_ref[...]
    ms = jnp.mean(x * x, axis=-1, keepdims=True)
    o_ref[...] = (x * lax.rsqrt(ms + EPS) * g_ref[...]).astype(o_ref.dtype)


def _rmsnorm(x, g, tm=512):
    t, d = x.shape
    return pl.pallas_call(
        _rmsnorm_kernel,
        grid=(t // tm,),
        in_specs=[pl.BlockSpec((tm, d), lambda i: (i, 0)),
                  pl.BlockSpec((1, d), lambda i: (0, 0))],
        out_specs=pl.BlockSpec((tm, d), lambda i: (i, 0)),
        out_shape=jax.ShapeDtypeStruct((t, d), BF16),
        compiler_params=_params(("parallel",)),
        name="rmsnorm",
    )(x, g.reshape(1, d))


CAST_ROWS = 256


def _cast_weight(w_ref, wb_ref):
    def chunk(i, carry):
        rows = pl.ds(pl.multiple_of(i * CAST_ROWS, CAST_ROWS), CAST_ROWS)
        wb_ref[rows, :] = w_ref[rows, :].astype(wb_ref.dtype)
        return carry

    lax.fori_loop(0, w_ref.shape[0] // CAST_ROWS, chunk, 0)


def _in_proj_kernel(a_ref, w_ref, o_ref, wb_ref):
    @pl.when(pl.program_id(1) == 0)
    def _():
        _cast_weight(w_ref, wb_ref)

    o_ref[...] = jnp.dot(a_ref[...], wb_ref[...],
                         preferred_element_type=F32).astype(o_ref.dtype)


def _in_proj(a, w_all, layer, tm=1024, tn=1024):
    m, k = a.shape
    n = w_all.shape[2]
    return pl.pallas_call(
        _in_proj_kernel,
        grid=(n // tn, m // tm),
        in_specs=[pl.BlockSpec((tm, k), lambda j, i: (i, 0)),
                  pl.BlockSpec((None, k, tn), lambda j, i: (layer, 0, j))],
        out_specs=pl.BlockSpec((tm, tn), lambda j, i: (i, j)),
        out_shape=jax.ShapeDtypeStruct((m, n), BF16),
        scratch_shapes=[pltpu.VMEM((k, tn), BF16)],
        compiler_params=_params(("parallel", "arbitrary")),
        name="in_proj",
    )(a, w_all)


POOL_PAD = 16
POOL_ROWS = 64


def _gapped(row_start, n_rows):
    return pl.ds(2 * row_start, n_rows, stride=2)


def _pool_kernel(z_ref, pw_ref, ps_ref, o_ref, xpad_ref, pwb_ref):
    s_len = z_ref.shape[0]
    width = len(POOL_WINDOWS) * POOL_CH
    n_slabs = width // LANES
    slabs_per_group = POOL_CH // LANES
    for g in range(len(POOL_WINDOWS)):
        pwb_ref[g] = pw_ref[g].astype(BF16)
    for c in range(n_slabs):
        xpad_ref[c, 0:2 * POOL_PAD, :] = jnp.zeros((2 * POOL_PAD, LANES), F32)

    def fill(i, carry):
        r = pl.multiple_of(i * POOL_ROWS, POOL_ROWS)
        for c in range(n_slabs):
            xpad_ref[c, _gapped(POOL_PAD + r, POOL_ROWS), :] = (
                z_ref[pl.ds(r, POOL_ROWS), c * LANES:(c + 1) * LANES].astype(F32))
        return carry

    lax.fori_loop(0, s_len // POOL_ROWS, fill, 0)

    def body(i, carry):
        r = pl.multiple_of(i * POOL_ROWS, POOL_ROWS)
        row = r + lax.broadcasted_iota(jnp.int32, (POOL_ROWS, POOL_CH), 0)
        for g, w in enumerate(POOL_WINDOWS):
            cs = slice(g * POOL_CH, (g + 1) * POOL_CH)
            xs, accs = [], []
            for c in range(g * slabs_per_group, (g + 1) * slabs_per_group):
                xc = xpad_ref[c, _gapped(POOL_PAD + r, POOL_ROWS), :]
                ac = xc
                for j in range(1, w):
                    ac = ac + xpad_ref[c, _gapped(POOL_PAD + r - j, POOL_ROWS), :]
                xs.append(xc)
                accs.append(ac)
            x = jnp.concatenate(xs, axis=-1)
            acc = jnp.concatenate(accs, axis=-1)
            cnt = jnp.minimum(row + 1, w).astype(F32)
            pooled = acc / cnt - x
            mixed = jnp.dot(pooled.astype(BF16), pwb_ref[g], preferred_element_type=F32)
            gate = z_ref[pl.ds(r, POOL_ROWS), width + g * POOL_CH: width + (g + 1) * POOL_CH].astype(F32)
            y = mixed * ps_ref[:, cs] * _silu(gate)
            o_ref[pl.ds(r, POOL_ROWS), cs] = y.astype(o_ref.dtype)
        return c

    lax.fori_loop(0, s_len // POOL_ROWS, body, 0)


def _pool_mixer(z3, pool_w, pool_scale):
    b, s, _ = z3.shape
    width = len(POOL_WINDOWS) * POOL_CH
    return pl.pallas_call(
        _pool_kernel,
        grid=(b,),
        in_specs=[pl.BlockSpec((None, s, 2 * width), lambda i: (i, 0, 0)),
                  pl.BlockSpec(pool_w.shape, lambda i: (0, 0, 0)),
                  pl.BlockSpec((1, width), lambda i: (0, 0))],
        out_specs=pl.BlockSpec((None, s, width), lambda i: (i, 0, 0)),
        out_shape=jax.ShapeDtypeStruct((b, s, width), BF16),
        scratch_shapes=[pltpu.VMEM((width // LANES, 2 * (POOL_PAD + s), LANES), F32),
                        pltpu.VMEM(pool_w.shape, BF16)],
        compiler_params=_params(("parallel",)),
        name="pool_mixer",
    )(z3, pool_w, pool_scale.reshape(1, width))


ATT_ROWS = 256
ATT_INTERLEAVE = 4
ATT_INTERLEAVE_DENSE = 5
assert all(d <= ATT_INTERLEAVE or d % ATT_INTERLEAVE == 0 for d in DILATIONS)


def _attn_block(q, k, v, mask):
    s = lax.dot_general(q, k, (((1,), (1,)), ((), ())), preferred_element_type=F32)
    s = jnp.where(mask, s * (HEAD_DIM ** -0.5), NEG)
    m = jnp.max(s, axis=-1, keepdims=True)
    p = jnp.exp(s - m)
    den = jnp.sum(p, axis=-1, keepdims=True)
    o = jnp.dot(p.astype(BF16), v, preferred_element_type=F32)
    return o / den, m + jnp.log(den)


def _attn_kernel(q0, k0, v0, q1, k1, v1, q2, k2, v2, gate_ref, cos_ref, sa_ref, sb_ref,
                 o_ref, qkv_ref, og_ref, lse_ref):
    s_len = o_ref.shape[0]
    qkv_in = ((q0, k0, v0), (q1, k1, v1), (q2, k2, v2))

    def prep(i, c):
        r = pl.multiple_of(i * ATT_ROWS, ATT_ROWS)
        rows = pl.ds(r, ATT_ROWS)
        cos, sa, sb = cos_ref[rows, :], sa_ref[rows, :], sb_ref[rows, :]
        for g in range(ATT_GROUPS):
            for j in range(2):
                t = qkv_in[g][j][rows, :].astype(F32)
                up = pltpu.roll(t, HEAD_DIM - ROT_HALF, 1)
                down = pltpu.roll(t, ROT_HALF, 1)
                qkv_ref[3 * g + j, rows, :] = t * cos + up * sa + down * sb
            qkv_ref[3 * g + 2, rows, :] = qkv_in[g][2][rows, :].astype(F32)
        return c

    lax.fori_loop(0, s_len // ATT_ROWS, prep, 0)

    qi = lax.broadcasted_iota(jnp.int32, (SPAN, 2 * SPAN), 0)
    kj = lax.broadcasted_iota(jnp.int32, (SPAN, 2 * SPAN), 1)
    band = (kj >= qi) & (kj <= qi + SPAN)
    causal = (lax.broadcasted_iota(jnp.int32, (SPAN, SPAN), 1)
              <= lax.broadcasted_iota(jnp.int32, (SPAN, SPAN), 0))

    def run_block(g, d, q_start, k_start, n_keys, mask):
        def rows(start, n):
            return pl.ds(start, n) if d == 1 else pl.ds(start, n, stride=d)
        q = qkv_ref[3 * g, rows(q_start, SPAN), :].astype(BF16)
        k = qkv_ref[3 * g + 1, rows(k_start, n_keys), :].astype(BF16)
        v = qkv_ref[3 * g + 2, rows(k_start, n_keys), :].astype(BF16)
        o, lse = _attn_block(q, k, v, mask)
        og_ref[g, rows(q_start, SPAN), :] = o
        lse_ref[g, rows(q_start, SPAN), :] = jnp.broadcast_to(lse, (SPAN, HEAD_DIM))

    for g, d in enumerate(DILATIONS):
        n_blocks = s_len // (d * SPAN)
        stride_blk = d * SPAN

        def first_blocks(it, carry, g=g, d=d):
            for u in range(min(d, ATT_INTERLEAVE)):
                r = it * ATT_INTERLEAVE + u
                run_block(g, d, r, r, SPAN, causal)
            return carry

        if d <= ATT_INTERLEAVE:
            first_blocks(0, 0)
        else:
            lax.fori_loop(0, d // ATT_INTERLEAVE, first_blocks, 0)

        rest = [(r, i) for i in range(1, n_blocks) for r in range(d)]
        if not rest:
            continue
        if d == 1:
            per_iter = max(u for u in range(1, ATT_INTERLEAVE_DENSE + 1) if (n_blocks - 1) % u == 0)

            def dense_blocks(it, carry, g=g):
                for u in range(per_iter):
                    q_start = pl.multiple_of((1 + it * per_iter + u) * SPAN, SPAN)
                    k_start = pl.multiple_of((it * per_iter + u) * SPAN, SPAN)
                    run_block(g, 1, q_start, k_start, 2 * SPAN, band)
                return carry

            lax.fori_loop(0, (n_blocks - 1) // per_iter, dense_blocks, 0)
        else:
            def strided_blocks(i, carry, g=g, d=d, stride_blk=stride_blk):
                for r in range(d):
                    q_start = r + i * stride_blk
                    run_block(g, d, q_start, q_start - stride_blk, 2 * SPAN, band)
                return carry

            lax.fori_loop(1, n_blocks, strided_blocks, 0)

    def merge(i, c):
        r = pl.multiple_of(i * ATT_ROWS, ATT_ROWS)
        rows = pl.ds(r, ATT_ROWS)
        l0, l1, l2 = lse_ref[0, rows, :], lse_ref[1, rows, :], lse_ref[2, rows, :]
        m = jnp.maximum(jnp.maximum(l0, l1), l2)
        e0, e1, e2 = jnp.exp(l0 - m), jnp.exp(l1 - m), jnp.exp(l2 - m)
        num = e0 * og_ref[0, rows, :] + e1 * og_ref[1, rows, :] + e2 * og_ref[2, rows, :]
        o = num / (e0 + e1 + e2)
        gate = gate_ref[rows, :].astype(F32)
        o_ref[rows, :] = (o * _silu(gate)).astype(o_ref.dtype)
        return c

    lax.fori_loop(0, s_len // ATT_ROWS, merge, 0)


def _rope_tables(s_len):
    pos = jnp.arange(s_len, dtype=F32)
    inv_freq = jnp.power(ROPE_THETA, -jnp.arange(0, ROT_DIM, 2, dtype=F32) / ROT_DIM)
    ang = pos[:, None] * inv_freq[None, :]
    cos, sin = jnp.cos(ang), jnp.sin(ang)
    zeros = jnp.zeros((s_len, HEAD_DIM - ROT_DIM), F32)
    zh = jnp.zeros((s_len, ROT_HALF), F32)
    cos_t = jnp.concatenate([cos, cos, jnp.ones((s_len, HEAD_DIM - ROT_DIM), F32)], axis=-1)
    sa_t = jnp.concatenate([-sin, zh, zeros], axis=-1)
    sb_t = jnp.concatenate([zh, sin, zeros], axis=-1)
    return cos_t, sa_t, sb_t


def _dilated_attention(z3, q_col, n_heads):
    b, s, _ = z3.shape
    per = ATT_GROUPS * n_heads
    qb = q_col // HEAD_DIM
    cos_t, sa_t, sb_t = _rope_tables(s)

    def head_spec(blk0):
        return pl.BlockSpec((None, s, HEAD_DIM), lambda i, h, blk0=blk0: (i, 0, blk0 + h))

    in_specs = []
    for g in range(ATT_GROUPS):
        for j in range(3):
            in_specs.append(head_spec(qb + j * per + g * n_heads))
    in_specs.append(head_spec(qb + 3 * per))
    tab_spec = pl.BlockSpec((s, HEAD_DIM), lambda i, h: (0, 0))
    in_specs += [tab_spec, tab_spec, tab_spec]
    return pl.pallas_call(
        _attn_kernel,
        grid=(b, n_heads),
        in_specs=in_specs,
        out_specs=pl.BlockSpec((None, s, HEAD_DIM), lambda i, h: (i, 0, h)),
        out_shape=jax.ShapeDtypeStruct((b, s, n_heads * HEAD_DIM), BF16),
        scratch_shapes=[pltpu.VMEM((3 * ATT_GROUPS, s, HEAD_DIM), F32),
                        pltpu.VMEM((ATT_GROUPS, s, HEAD_DIM), F32),
                        pltpu.VMEM((ATT_GROUPS, s, HEAD_DIM), F32)],
        compiler_params=_params(("parallel", "parallel")),
        name="dilated_attention",
    )(*([z3] * 10), cos_t, sa_t, sb_t)


def _out_proj_kernel(ya_ref, yb_ref, w_ref, x_ref, gpost_ref, *rest, with_next):
    if with_next:
        gnext_ref, xo_ref, ho_ref, wb_ref = rest
    else:
        xo_ref, wb_ref = rest

    @pl.when(pl.program_id(0) == 0)
    def _():
        _cast_weight(w_ref, wb_ref)

    half = ya_ref.shape[1]
    y = jnp.dot(ya_ref[...], wb_ref[0:half, :], preferred_element_type=F32)
    y = y + jnp.dot(yb_ref[...], wb_ref[half:2 * half, :], preferred_element_type=F32)
    yn = y * lax.rsqrt(jnp.mean(y * y, axis=-1, keepdims=True) + EPS) * gpost_ref[...]
    xn = x_ref[...] + yn
    xo_ref[...] = xn
    if with_next:
        hn = xn * lax.rsqrt(jnp.mean(xn * xn, axis=-1, keepdims=True) + EPS) * gnext_ref[...]
        ho_ref[...] = hn.astype(ho_ref.dtype)


def _out_proj(ya, yb, w_all, layer, x, g_post, g_next=None, tm=256):
    t, half = ya.shape
    _, k, d = w_all.shape
    with_next = g_next is not None
    row = lambda i: (i, 0)
    fixed = lambda i: (0, 0)
    in_specs = [pl.BlockSpec((tm, half), row), pl.BlockSpec((tm, half), row),
                pl.BlockSpec((None, k, d), lambda i: (layer, 0, 0), pipeline_mode=pl.Buffered(1)),
                pl.BlockSpec((tm, d), row),
                pl.BlockSpec((1, d), fixed)]
    args = [ya, yb, w_all, x, g_post.reshape(1, d)]
    out_specs = [pl.BlockSpec((tm, d), row)]
    out_shape = [jax.ShapeDtypeStruct((t, d), F32)]
    if with_next:
        in_specs.append(pl.BlockSpec((1, d), fixed))
        args.append(g_next.reshape(1, d))
        out_specs.append(pl.BlockSpec((tm, d), row))
        out_shape.append(jax.ShapeDtypeStruct((t, d), BF16))
    return pl.pallas_call(
        functools.partial(_out_proj_kernel, with_next=with_next),
        grid=(t // tm,),
        in_specs=in_specs,
        out_specs=out_specs,
        out_shape=out_shape,
        scratch_shapes=[pltpu.VMEM((k, d), BF16)],
        compiler_params=_params(("arbitrary",)),
        name="out_proj",
    )(*args)


def _sgu_kernel(u_ref, v_ref, cg_ref, g_ref, b_ref, ws_ref, bias_ref, o_ref):
    tm = u_ref.shape[0]
    ii = lax.broadcasted_iota(jnp.int32, (CHUNK, CHUNK), 0)
    jj = lax.broadcasted_iota(jnp.int32, (CHUNK, CHUNK), 1)
    tril = jj <= ii
    for c in range(tm // CHUNK):
        rows = slice(c * CHUNK, (c + 1) * CHUNK)
        v = v_ref[rows, :].astype(F32)
        mu = jnp.mean(v, axis=-1, keepdims=True)
        vc = v - mu
        var = jnp.mean(vc * vc, axis=-1, keepdims=True)
        vn = (vc * lax.rsqrt(var + EPS) * g_ref[...] + b_ref[...]).astype(BF16)
        for h in range(SGU_GROUPS):
            cs = slice(h * SGU_CH, (h + 1) * SGU_CH)
            wm = jnp.where(tril, ws_ref[h], 0.0).astype(BF16)
            sg = jnp.dot(wm, vn[:, cs], preferred_element_type=F32) + bias_ref[:, cs]
            y = u_ref[rows, cs].astype(F32) * sg * _silu(cg_ref[rows, cs].astype(F32))
            o_ref[rows, cs] = y.astype(o_ref.dtype)


def _sgu(z, sgu_g, sgu_b, sgu_w, sgu_bias, tm=512):
    t = z.shape[0]
    width = SGU_GROUPS * SGU_CH
    bias_full = jnp.repeat(sgu_bias.T, SGU_CH, axis=1)
    fixed2 = lambda i: (0, 0)
    return pl.pallas_call(
        _sgu_kernel,
        grid=(t // tm,),
        in_specs=[pl.BlockSpec((tm, width), lambda i: (i, 0)),
                  pl.BlockSpec((tm, width), lambda i: (i, 1)),
                  pl.BlockSpec((tm, width), lambda i: (i, 2)),
                  pl.BlockSpec((1, width), fixed2), pl.BlockSpec((1, width), fixed2),
                  pl.BlockSpec(sgu_w.shape, lambda i: (0, 0, 0)),
                  pl.BlockSpec((CHUNK, width), fixed2)],
        out_specs=pl.BlockSpec((tm, width), lambda i: (i, 0)),
        out_shape=jax.ShapeDtypeStruct((t, width), BF16),
        compiler_params=_params(("parallel",)),
        name="sgu",
    )(z, z, z, sgu_g.reshape(1, width), sgu_b.reshape(1, width), sgu_w, bias_full)


CONV_PAD = 32
CONV_ROWS = 128
CONV_FILL = 128


def _conv_kernel(dv_ref, dg_ref, w_ref, cb_ref, o_ref, dpad_ref):
    s_len = dv_ref.shape[0]
    dpad_ref[0:2 * CONV_PAD, :] = jnp.zeros((2 * CONV_PAD, LANES), F32)

    def fill(i, carry):
        r = pl.multiple_of(i * CONV_FILL, CONV_FILL)
        rows = pl.ds(r, CONV_FILL)
        d = dv_ref[rows, :].astype(F32) * jax.nn.sigmoid(dg_ref[rows, :].astype(F32))
        dpad_ref[_gapped(CONV_PAD + r, CONV_FILL), :] = d
        return carry

    lax.fori_loop(0, s_len // CONV_FILL, fill, 0)

    lead = CONV_PAD - (CONV_K - 1)

    def body(i, carry):
        r = pl.multiple_of(i * CONV_ROWS, CONV_ROWS)
        acc = jnp.zeros((CONV_ROWS, LANES), F32)
        for k in range(CONV_K):
            acc = acc + dpad_ref[_gapped(r + lead + k, CONV_ROWS), :] * w_ref[k:k + 1, :]
        o_ref[pl.ds(r, CONV_ROWS), :] = acc + cb_ref[...]
        return carry

    lax.fori_loop(0, s_len // CONV_ROWS, body, 0)


def _conv_norm_kernel(c_ref, gate_ref, ng_ref, nb_ref, o_ref):
    acc = c_ref[...]
    mu = jnp.mean(acc, axis=-1, keepdims=True)
    ac = acc - mu
    var = jnp.mean(ac * ac, axis=-1, keepdims=True)
    y = _silu(ac * lax.rsqrt(var + EPS) * ng_ref[...] + nb_ref[...])
    o_ref[...] = (y * _silu(gate_ref[...].astype(F32))).astype(o_ref.dtype)


def _conv_module(z3, col, conv_w, conv_b, cn_g, cn_b, tm=512):
    b, s, cols = z3.shape
    ch = conv_w.shape[1]
    n_slabs = ch // LANES
    blk = col // LANES
    vec = lambda a: a.reshape(1, ch)
    conv = pl.pallas_call(
        _conv_kernel,
        grid=(b, n_slabs),
        in_specs=[pl.BlockSpec((None, s, LANES), lambda i, c: (i, 0, blk + c)),
                  pl.BlockSpec((None, s, LANES), lambda i, c: (i, 0, blk + n_slabs + c)),
                  pl.BlockSpec((CONV_K, LANES), lambda i, c: (0, c)),
                  pl.BlockSpec((1, LANES), lambda i, c: (0, c))],
        out_specs=pl.BlockSpec((None, s, LANES), lambda i, c: (i, 0, c)),
        out_shape=jax.ShapeDtypeStruct((b, s, ch), F32),
        scratch_shapes=[pltpu.VMEM((2 * (CONV_PAD + s), LANES), F32)],
        compiler_params=_params(("parallel", "parallel")),
        name="conv_taps",
    )(z3, z3, conv_w, vec(conv_b))
    t = b * s
    fixed2 = lambda i: (0, 0)
    gate_blk = col // ch + 2
    return pl.pallas_call(
        _conv_norm_kernel,
        grid=(t // tm,),
        in_specs=[pl.BlockSpec((tm, ch), lambda i: (i, 0)),
                  pl.BlockSpec((tm, ch), lambda i: (i, gate_blk)),
                  pl.BlockSpec((1, ch), fixed2), pl.BlockSpec((1, ch), fixed2)],
        out_specs=pl.BlockSpec((tm, ch), lambda i: (i, 0)),
        out_shape=jax.ShapeDtypeStruct((t, ch), BF16),
        compiler_params=_params(("parallel",)),
        name="conv_norm",
    )(conv.reshape(t, ch), z3.reshape(t, cols), vec(cn_g), vec(cn_b))


def kernel(x, e_pre_norm, e_w_in, e_pool_w, e_pool_scale, e_w_out, e_post_norm, o_pre_norm, o_w_in, o_sgu_norm_g, o_sgu_norm_b, o_sgu_w, o_sgu_b, o_conv_w, o_conv_b, o_conv_norm_g, o_conv_norm_b, o_w_out, o_post_norm):
    b, s, d = x.shape
    t = b * s
    n_even, n_odd = e_w_in.shape[0], o_w_in.shape[0]
    depth = n_even + n_odd
    pool_width = len(POOL_WINDOWS) * POOL_CH
    n_heads = (e_w_out.shape[1] - pool_width) // HEAD_DIM
    sgu_width = SGU_GROUPS * SGU_CH

    def pre_gain(i):
        return e_pre_norm[i // 2] if i % 2 == 0 else o_pre_norm[i // 2]

    xf = x.reshape(t, d)
    h = _rmsnorm(xf, pre_gain(0))
    for i in range(depth):
        j = i // 2
        g_next = pre_gain(i + 1) if i + 1 < depth else None
        if i % 2 == 0:
            z = _in_proj(h, e_w_in, j)
            z3 = z.reshape(b, s, z.shape[1])
            ya = _pool_mixer(z3, e_pool_w[j], e_pool_scale[j])
            yb = _dilated_attention(z3, 2 * pool_width, n_heads)
            w_out, g_post = e_w_out, e_post_norm[j]
        else:
            z = _in_proj(h, o_w_in, j)
            z3 = z.reshape(b, s, z.shape[1])
            ya = _sgu(z, o_sgu_norm_g[j], o_sgu_norm_b[j], o_sgu_w[j], o_sgu_b[j])
            yb = _conv_module(z3, 3 * sgu_width, o_conv_w[j], o_conv_b[j],
                              o_conv_norm_g[j], o_conv_norm_b[j])
            w_out, g_post = o_w_out, o_post_norm[j]
        outs = _out_proj(ya.reshape(t, -1), yb.reshape(t, -1), w_out, j, xf, g_post, g_next)
        if g_next is None:
            xf = outs[0]
        else:
            xf, h = outs
    return xf.reshape(b, s, d)
```

```python
import functools

import jax
import jax.numpy as jnp
import numpy as np
from jax import lax
from jax.experimental import pallas as pl
from jax.experimental.pallas import tpu as pltpu

F32 = jnp.float32
BF16 = jnp.bfloat16

EPS = 1e-6
NEG = -1e30
HEAD_DIM = 128
ROT_DIM = HEAD_DIM // 4
ROT_HALF = ROT_DIM // 2
ROPE_THETA = 500000.0
DILATIONS = (1, 4, 16)
SPAN = 128
ATT_GROUPS = len(DILATIONS)
POOL_WINDOWS = (2, 4, 8, 16)
POOL_CH = 256
SGU_GROUPS = 4
SGU_CH = 256
CHUNK = 128
CONV_K = 31

LANES = 128
MXU_COLS = 256
VMEM_LIMIT = 56 * 1024 * 1024


def _params(sem, vmem=VMEM_LIMIT):
    return pltpu.CompilerParams(dimension_semantics=sem, vmem_limit_bytes=vmem)


def _silu(x):
    return x * jax.nn.sigmoid(x)


def _rmsnorm_kernel(x_ref, g_ref, o_ref):
    x = x_ref[...]
    ms = jnp.mean(x * x, axis=-1, keepdims=True)
    o_ref[...] = (x * lax.rsqrt(ms + EPS) * g_ref[...]).astype(o_ref.dtype)


def _rmsnorm(x, g, tm=512):
    t, d = x.shape
    return pl.pallas_call(
        _rmsnorm_kernel,
        grid=(t // tm,),
        in_specs=[pl.BlockSpec((tm, d), lambda i: (i, 0)),
                  pl.BlockSpec((1, d), lambda i: (0, 0))],
        out_specs=pl.BlockSpec((tm, d), lambda i: (i, 0)),
        out_shape=jax.ShapeDtypeStruct((t, d), BF16),
        compiler_params=_params(("parallel",)),
        name="rmsnorm",
    )(x, g.reshape(1, d))


CAST_ROWS = 256


def _cast_weight(w_ref, wb_ref):
    def chunk(i, carry):
        rows = pl.ds(pl.multiple_of(i * CAST_ROWS, CAST_ROWS), CAST_ROWS)
        wb_ref[rows, :] = w_ref[rows, :].astype(wb_ref.dtype)
        return carry

    lax.fori_loop(0, w_ref.shape[0] // CAST_ROWS, chunk, 0)


def _in_proj_kernel(a_ref, w_ref, o_ref, wb_ref):
    @pl.when(pl.program_id(1) == 0)
    def _():
        _cast_weight(w_ref, wb_ref)

    o_ref[...] = jnp.dot(a_ref[...], wb_ref[...],
                         preferred_element_type=F32).astype(o_ref.dtype)


def _in_proj(a, w_all, layer, tm=1024, tn=1024):
    m, k = a.shape
    n = w_all.shape[2]
    return pl.pallas_call(
        _in_proj_kernel,
        grid=(n // tn, m // tm),
        in_specs=[pl.BlockSpec((tm, k), lambda j, i: (i, 0)),
                  pl.BlockSpec((None, k, tn), lambda j, i: (layer, 0, j))],
        out_specs=pl.BlockSpec((tm, tn), lambda j, i: (i, j)),
        out_shape=jax.ShapeDtypeStruct((m, n), BF16),
        scratch_shapes=[pltpu.VMEM((k, tn), BF16)],
        compiler_params=_params(("parallel", "arbitrary")),
        name="in_proj",
    )(a, w_all)


POOL_PAD = 16
POOL_ROWS = 64


def _gapped(row_start, n_rows):
    return pl.ds(2 * row_start, n_rows, stride=2)


def _pool_kernel(z_ref, pw_ref, ps_ref, o_ref, xpad_ref, pwb_ref):
    s_len = z_ref.shape[0]
    width = len(POOL_WINDOWS) * POOL_CH
    n_slabs = width // LANES
    slabs_per_group = POOL_CH // LANES
    for g in range(len(POOL_WINDOWS)):
        pwb_ref[g] = pw_ref[g].astype(BF16)
    for c in range(n_slabs):
        xpad_ref[c, 0:2 * POOL_PAD, :] = jnp.zeros((2 * POOL_PAD, LANES), F32)

    def fill(i, carry):
        r = pl.multiple_of(i * POOL_ROWS, POOL_ROWS)
        for c in range(n_slabs):
            xpad_ref[c, _gapped(POOL_PAD + r, POOL_ROWS), :] = (
                z_ref[pl.ds(r, POOL_ROWS), c * LANES:(c + 1) * LANES].astype(F32))
        return carry

    lax.fori_loop(0, s_len // POOL_ROWS, fill, 0)

    def body(i, carry):
        r = pl.multiple_of(i * POOL_ROWS, POOL_ROWS)
        row = r + lax.broadcasted_iota(jnp.int32, (POOL_ROWS, POOL_CH), 0)
        for g, w in enumerate(POOL_WINDOWS):
            cs = slice(g * POOL_CH, (g + 1) * POOL_CH)
            xs, accs = [], []
            for c in range(g * slabs_per_group, (g + 1) * slabs_per_group):
                xc = xpad_ref[c, _gapped(POOL_PAD + r, POOL_ROWS), :]
                ac = xc
                for j in range(1, w):
                    ac = ac + xpad_ref[c, _gapped(POOL_PAD + r - j, POOL_ROWS), :]
                xs.append(xc)
                accs.append(ac)
            x = jnp.concatenate(xs, axis=-1)
            acc = jnp.concatenate(accs, axis=-1)
            cnt = jnp.minimum(row + 1, w).astype(F32)
            pooled = acc / cnt - x
            mixed = jnp.dot(pooled.astype(BF16), pwb_ref[g], preferred_element_type=F32)
            gate = z_ref[pl.ds(r, POOL_ROWS), width + g * POOL_CH: width + (g + 1) * POOL_CH].astype(F32)
            y = mixed * ps_ref[:, cs] * _silu(gate)
            o_ref[pl.ds(r, POOL_ROWS), cs] = y.astype(o_ref.dtype)
        return c

    lax.fori_loop(0, s_len // POOL_ROWS, body, 0)


def _pool_mixer(z3, pool_w, pool_scale):
    b, s, _ = z3.shape
    width = len(POOL_WINDOWS) * POOL_CH
    return pl.pallas_call(
        _pool_kernel,
        grid=(b,),
        in_specs=[pl.BlockSpec((None, s, 2 * width), lambda i: (i, 0, 0)),
                  pl.BlockSpec(pool_w.shape, lambda i: (0, 0, 0)),
                  pl.BlockSpec((1, width), lambda i: (0, 0))],
        out_specs=pl.BlockSpec((None, s, width), lambda i: (i, 0, 0)),
        out_shape=jax.ShapeDtypeStruct((b, s, width), BF16),
        scratch_shapes=[pltpu.VMEM((width // LANES, 2 * (POOL_PAD + s), LANES), F32),
                        pltpu.VMEM(pool_w.shape, BF16)],
        compiler_params=_params(("parallel",)),
        name="pool_mixer",
    )(z3, pool_w, pool_scale.reshape(1, width))


ATT_ROWS = 256
ATT_MAX_BATCH = 16
ATT_INTERLEAVE = 5
assert all(d <= ATT_MAX_BATCH for d in DILATIONS)


def _largest_divisor(n, cap):
    return max(u for u in range(1, cap + 1) if n % u == 0)


def _for_blocks(n, fn):
    per = _largest_divisor(n, ATT_INTERLEAVE)

    def body(it, carry):
        for u in range(per):
            fn(it * per + u)
        return carry

    if n == per:
        body(0, 0)
    else:
        lax.fori_loop(0, n // per, body, 0)


def _aligned(start, align):
    return start if isinstance(start, int) else pl.multiple_of(start, align)


def _attn_kernel(q0, k0, v0, q1, k1, v1, q2, k2, v2, gate_ref, cos_ref, sin_ref,
                 o_ref, qkv_ref, s_ref, ms_ref, acc_ref, m_ref, l_ref):
    s_len = o_ref.shape[0]
    qkv_in = ((q0, k0, v0), (q1, k1, v1), (q2, k2, v2))

    src = lax.broadcasted_iota(jnp.int32, (HEAD_DIM, HEAD_DIM), 0)
    dst = lax.broadcasted_iota(jnp.int32, (HEAD_DIM, HEAD_DIM), 1)
    perm = (jnp.where((dst < ROT_HALF) & (src == dst + ROT_HALF), -1.0, 0.0)
            + jnp.where((dst >= ROT_HALF) & (dst < ROT_DIM) & (src == dst - ROT_HALF), 1.0, 0.0)
            ).astype(BF16)

    def prep(i, c):
        rows = pl.ds(pl.multiple_of(i * ATT_ROWS, ATT_ROWS), ATT_ROWS)
        cos, sin = cos_ref[rows, :], sin_ref[rows, :]
        for g in range(ATT_GROUPS):
            for j in range(2):
                t = qkv_in[g][j][rows, :]
                partner = jnp.dot(t, perm, preferred_element_type=F32)
                qkv_ref[3 * g + j, rows, :] = t.astype(F32) * cos + partner * sin
            qkv_ref[3 * g + 2, rows, :] = qkv_in[g][2][rows, :].astype(F32)
        return c

    lax.fori_loop(0, s_len // ATT_ROWS, prep, 0)

    qi = lax.broadcasted_iota(jnp.int32, (SPAN, 2 * SPAN), 0)
    kj = lax.broadcasted_iota(jnp.int32, (SPAN, 2 * SPAN), 1)
    band = (kj >= qi) & (kj <= qi + SPAN)
    causal = (lax.broadcasted_iota(jnp.int32, (SPAN, SPAN), 1)
              <= lax.broadcasted_iota(jnp.int32, (SPAN, SPAN), 0))

    def run_batch(g, d, n, n_keys, mask, starts):
        assert n <= ATT_MAX_BATCH

        def rows(start, cnt):
            return pl.ds(_aligned(start, SPAN), cnt) if d == 1 else pl.ds(start, cnt, stride=d)

        def operand(j, start, cnt):
            return qkv_ref[3 * g + j, rows(start, cnt), :].astype(BF16)

        def blk(j):
            return pl.ds(_aligned(j * SPAN, SPAN), SPAN)

        def scores(j):
            q_start, k_start = starts(j)
            s = lax.dot_general(operand(0, q_start, SPAN), operand(1, k_start, n_keys),
                                (((1,), (1,)), ((), ())), preferred_element_type=F32)
            s = jnp.where(mask, s * (HEAD_DIM ** -0.5), NEG)
            s_ref[blk(j), 0:n_keys] = s
            m = jnp.max(s, axis=-1, keepdims=True)
            ms_ref[blk(j), :] = jnp.broadcast_to(m, (SPAN, HEAD_DIM))

        def outputs(j):
            q_start, k_start = starts(j)
            m = ms_ref[blk(j), :]
            den, ps = None, []
            for c in range(n_keys // HEAD_DIM):
                p = jnp.exp(s_ref[blk(j), c * HEAD_DIM:(c + 1) * HEAD_DIM] - m)
                ps.append(p.astype(BF16))
                den = p if den is None else den + p
            den = jnp.sum(den, axis=-1, keepdims=True)
            m_ref[g, rows(q_start, SPAN), :] = m
            l_ref[g, rows(q_start, SPAN), :] = jnp.broadcast_to(den, (SPAN, HEAD_DIM))
            pb = ps[0] if len(ps) == 1 else jnp.concatenate(ps, axis=1)
            acc_ref[g, rows(q_start, SPAN), :] = jnp.dot(pb, operand(2, k_start, n_keys),
                                                        preferred_element_type=F32)

        for stage in (scores, outputs):
            _for_blocks(n, stage)

    for g, d in enumerate(DILATIONS):
        n_blocks = s_len // (d * SPAN)
        stride_blk = d * SPAN

        run_batch(g, d, d, SPAN, causal, lambda j: (j, j))

        def starts(j, d=d, stride_blk=stride_blk):
            if d == 1:
                return (1 + j) * SPAN, j * SPAN
            if isinstance(j, int):
                r, i = j % d, 1 + j // d
            else:
                r, i = lax.rem(j, d), 1 + lax.div(j, d)
            q_start = r + i * stride_blk
            return q_start, q_start - stride_blk

        if n_blocks > 1:
            run_batch(g, d, d * (n_blocks - 1), 2 * SPAN, band, starts)

    def merge(i, c):
        r = pl.multiple_of(i * ATT_ROWS, ATT_ROWS)
        rows = pl.ds(r, ATT_ROWS)
        ms = [m_ref[g, rows, :] for g in range(ATT_GROUPS)]
        m = functools.reduce(jnp.maximum, ms)
        num = den = None
        for g in range(ATT_GROUPS):
            e = jnp.exp(ms[g] - m)
            ng, dg = e * acc_ref[g, rows, :], e * l_ref[g, rows, :]
            num, den = (ng, dg) if num is None else (num + ng, den + dg)
        gate = gate_ref[rows, :].astype(F32)
        o_ref[rows, :] = (num / den * _silu(gate)).astype(o_ref.dtype)
        return c

    lax.fori_loop(0, s_len // ATT_ROWS, merge, 0)


def _rope_tables(s_len):
    pos = jnp.arange(s_len, dtype=F32)
    inv_freq = jnp.power(ROPE_THETA, -jnp.arange(0, ROT_DIM, 2, dtype=F32) / ROT_DIM)
    ang = pos[:, None] * inv_freq[None, :]
    cos, sin = jnp.cos(ang), jnp.sin(ang)
    rest = jnp.zeros((s_len, HEAD_DIM - ROT_DIM), F32)
    return (jnp.concatenate([cos, cos, rest + 1.0], axis=-1),
            jnp.concatenate([sin, sin, rest], axis=-1))


def _dilated_attention(z3, q_col, n_heads):
    b, s, _ = z3.shape
    per = ATT_GROUPS * n_heads
    qb = q_col // HEAD_DIM
    cos_t, sin_t = _rope_tables(s)

    def head_spec(blk0):
        return pl.BlockSpec((None, s, HEAD_DIM), lambda i, h, blk0=blk0: (i, 0, blk0 + h))

    in_specs = []
    for g in range(ATT_GROUPS):
        for j in range(3):
            in_specs.append(head_spec(qb + j * per + g * n_heads))
    in_specs.append(head_spec(qb + 3 * per))
    tab_spec = pl.BlockSpec((s, HEAD_DIM), lambda i, h: (0, 0))
    in_specs += [tab_spec, tab_spec]
    return pl.pallas_call(
        _attn_kernel,
        grid=(b, n_heads),
        in_specs=in_specs,
        out_specs=pl.BlockSpec((None, s, HEAD_DIM), lambda i, h: (i, 0, h)),
        out_shape=jax.ShapeDtypeStruct((b, s, n_heads * HEAD_DIM), BF16),
        scratch_shapes=[pltpu.VMEM((3 * ATT_GROUPS, s, HEAD_DIM), F32),
                        pltpu.VMEM((ATT_MAX_BATCH * SPAN, 2 * SPAN), F32),
                        pltpu.VMEM((ATT_MAX_BATCH * SPAN, HEAD_DIM), F32),
                        pltpu.VMEM((ATT_GROUPS, s, HEAD_DIM), F32),
                        pltpu.VMEM((ATT_GROUPS, s, HEAD_DIM), F32),
                        pltpu.VMEM((ATT_GROUPS, s, HEAD_DIM), F32)],
        compiler_params=_params(("parallel", "parallel")),
        name="dilated_attention",
    )(*([z3] * 10), cos_t, sin_t)


def _out_proj_kernel(ya_ref, yb_ref, w_ref, x_ref, gpost_ref, *rest, with_next):
    if with_next:
        gnext_ref, xo_ref, ho_ref, wb_ref = rest
    else:
        xo_ref, wb_ref = rest

    @pl.when(pl.program_id(0) == 0)
    def _():
        _cast_weight(w_ref, wb_ref)

    half = ya_ref.shape[1]
    y = jnp.dot(ya_ref[...], wb_ref[0:half, :], preferred_element_type=F32)
    y = y + jnp.dot(yb_ref[...], wb_ref[half:2 * half, :], preferred_element_type=F32)
    yn = y * lax.rsqrt(jnp.mean(y * y, axis=-1, keepdims=True) + EPS) * gpost_ref[...]
    xn = x_ref[...] + yn
    xo_ref[...] = xn
    if with_next:
        hn = xn * lax.rsqrt(jnp.mean(xn * xn, axis=-1, keepdims=True) + EPS) * gnext_ref[...]
        ho_ref[...] = hn.astype(ho_ref.dtype)


def _out_proj(ya, yb, w_all, layer, x, g_post, g_next=None, tm=256):
    t, half = ya.shape
    _, k, d = w_all.shape
    with_next = g_next is not None
    row = lambda i: (i, 0)
    fixed = lambda i: (0, 0)
    in_specs = [pl.BlockSpec((tm, half), row), pl.BlockSpec((tm, half), row),
                pl.BlockSpec((None, k, d), lambda i: (layer, 0, 0), pipeline_mode=pl.Buffered(1)),
                pl.BlockSpec((tm, d), row),
                pl.BlockSpec((1, d), fixed)]
    args = [ya, yb, w_all, x, g_post.reshape(1, d)]
    out_specs = [pl.BlockSpec((tm, d), row)]
    out_shape = [jax.ShapeDtypeStruct((t, d), F32)]
    if with_next:
        in_specs.append(pl.BlockSpec((1, d), fixed))
        args.append(g_next.reshape(1, d))
        out_specs.append(pl.BlockSpec((tm, d), row))
        out_shape.append(jax.ShapeDtypeStruct((t, d), BF16))
    return pl.pallas_call(
        functools.partial(_out_proj_kernel, with_next=with_next),
        grid=(t // tm,),
        in_specs=in_specs,
        out_specs=out_specs,
        out_shape=out_shape,
        scratch_shapes=[pltpu.VMEM((k, d), BF16)],
        compiler_params=_params(("arbitrary",)),
        name="out_proj",
    )(*args)


def _sgu_kernel(u_ref, v_ref, cg_ref, g_ref, b_ref, ws_ref, bias_ref, o_ref):
    tm = u_ref.shape[0]
    ii = lax.broadcasted_iota(jnp.int32, (CHUNK, CHUNK), 0)
    jj = lax.broadcasted_iota(jnp.int32, (CHUNK, CHUNK), 1)
    tril = jj <= ii
    for c in range(tm // CHUNK):
        rows = slice(c * CHUNK, (c + 1) * CHUNK)
        v = v_ref[rows, :].astype(F32)
        mu = jnp.mean(v, axis=-1, keepdims=True)
        vc = v - mu
        var = jnp.mean(vc * vc, axis=-1, keepdims=True)
        vn = (vc * lax.rsqrt(var + EPS) * g_ref[...] + b_ref[...]).astype(BF16)
        for h in range(SGU_GROUPS):
            cs = slice(h * SGU_CH, (h + 1) * SGU_CH)
            wm = jnp.where(tril, ws_ref[h], 0.0).astype(BF16)
            sg = jnp.dot(wm, vn[:, cs], preferred_element_type=F32) + bias_ref[:, cs]
            y = u_ref[rows, cs].astype(F32) * sg * _silu(cg_ref[rows, cs].astype(F32))
            o_ref[rows, cs] = y.astype(o_ref.dtype)


def _sgu(z, sgu_g, sgu_b, sgu_w, sgu_bias, tm=512):
    t = z.shape[0]
    width = SGU_GROUPS * SGU_CH
    bias_full = jnp.repeat(sgu_bias.T, SGU_CH, axis=1)
    fixed2 = lambda i: (0, 0)
    return pl.pallas_call(
        _sgu_kernel,
        grid=(t // tm,),
        in_specs=[pl.BlockSpec((tm, width), lambda i: (i, 0)),
                  pl.BlockSpec((tm, width), lambda i: (i, 1)),
                  pl.BlockSpec((tm, width), lambda i: (i, 2)),
                  pl.BlockSpec((1, width), fixed2), pl.BlockSpec((1, width), fixed2),
                  pl.BlockSpec(sgu_w.shape, lambda i: (0, 0, 0)),
                  pl.BlockSpec((CHUNK, width), fixed2)],
        out_specs=pl.BlockSpec((tm, width), lambda i: (i, 0)),
        out_shape=jax.ShapeDtypeStruct((t, width), BF16),
        compiler_params=_params(("parallel",)),
        name="sgu",
    )(z, z, z, sgu_g.reshape(1, width), sgu_b.reshape(1, width), sgu_w, bias_full)


CONV_PAD = 32
CONV_ROWS = 128
CONV_FILL = 128


def _conv_kernel(dv_ref, dg_ref, w_ref, cb_ref, o_ref, dpad_ref):
    s_len = dv_ref.shape[0]
    dpad_ref[0:CONV_PAD, :] = jnp.zeros((CONV_PAD, LANES), F32)

    def fill(i, carry):
        r = pl.multiple_of(i * CONV_FILL, CONV_FILL)
        rows = pl.ds(r, CONV_FILL)
        d = dv_ref[rows, :].astype(F32) * jax.nn.sigmoid(dg_ref[rows, :].astype(F32))
        dpad_ref[pl.ds(CONV_PAD + r, CONV_FILL), :] = d
        return carry

    lax.fori_loop(0, s_len // CONV_FILL, fill, 0)

    lead = CONV_PAD - (CONV_K - 1)
    taps = [w_ref[k:k + 1, :] for k in range(CONV_K)]
    for r in range(0, s_len, CONV_ROWS):
        acc = dpad_ref[r + lead:r + lead + CONV_ROWS, :] * taps[0]
        for k in range(1, CONV_K):
            acc = acc + dpad_ref[r + lead + k:r + lead + k + CONV_ROWS, :] * taps[k]
        o_ref[r:r + CONV_ROWS, :] = acc + cb_ref[...]


def _conv_norm_kernel(c_ref, gate_ref, ng_ref, nb_ref, o_ref):
    acc = c_ref[...]
    mu = jnp.mean(acc, axis=-1, keepdims=True)
    ac = acc - mu
    var = jnp.mean(ac * ac, axis=-1, keepdims=True)
    y = _silu(ac * lax.rsqrt(var + EPS) * ng_ref[...] + nb_ref[...])
    o_ref[...] = (y * _silu(gate_ref[...].astype(F32))).astype(o_ref.dtype)


def _conv_module(z3, col, conv_w, conv_b, cn_g, cn_b, tm=512):
    b, s, cols = z3.shape
    ch = conv_w.shape[1]
    n_slabs = ch // LANES
    blk = col // LANES
    vec = lambda a: a.reshape(1, ch)
    conv = pl.pallas_call(
        _conv_kernel,
        grid=(b, n_slabs),
        in_specs=[pl.BlockSpec((None, s, LANES), lambda i, c: (i, 0, blk + c)),
                  pl.BlockSpec((None, s, LANES), lambda i, c: (i, 0, blk + n_slabs + c)),
                  pl.BlockSpec((CONV_K, LANES), lambda i, c: (0, c)),
                  pl.BlockSpec((1, LANES), lambda i, c: (0, c))],
        out_specs=pl.BlockSpec((None, s, LANES), lambda i, c: (i, 0, c)),
        out_shape=jax.ShapeDtypeStruct((b, s, ch), F32),
        scratch_shapes=[pltpu.VMEM((CONV_PAD + s, LANES), F32)],
        compiler_params=_params(("parallel", "parallel")),
        name="conv_taps",
    )(z3, z3, conv_w, vec(conv_b))
    t = b * s
    fixed2 = lambda i: (0, 0)
    gate_blk = col // ch + 2
    return pl.pallas_call(
        _conv_norm_kernel,
        grid=(t // tm,),
        in_specs=[pl.BlockSpec((tm, ch), lambda i: (i, 0)),
                  pl.BlockSpec((tm, ch), lambda i: (i, gate_blk)),
                  pl.BlockSpec((1, ch), fixed2), pl.BlockSpec((1, ch), fixed2)],
        out_specs=pl.BlockSpec((tm, ch), lambda i: (i, 0)),
        out_shape=jax.ShapeDtypeStruct((t, ch), BF16),
        compiler_params=_params(("parallel",)),
        name="conv_norm",
    )(conv.reshape(t, ch), z3.reshape(t, cols), vec(cn_g), vec(cn_b))


def kernel(x, e_pre_norm, e_w_in, e_pool_w, e_pool_scale, e_w_out, e_post_norm, o_pre_norm, o_w_in, o_sgu_norm_g, o_sgu_norm_b, o_sgu_w, o_sgu_b, o_conv_w, o_conv_b, o_conv_norm_g, o_conv_norm_b, o_w_out, o_post_norm):
    b, s, d = x.shape
    t = b * s
    n_even, n_odd = e_w_in.shape[0], o_w_in.shape[0]
    depth = n_even + n_odd
    pool_width = len(POOL_WINDOWS) * POOL_CH
    n_heads = (e_w_out.shape[1] - pool_width) // HEAD_DIM
    sgu_width = SGU_GROUPS * SGU_CH

    def pre_gain(i):
        return e_pre_norm[i // 2] if i % 2 == 0 else o_pre_norm[i // 2]

    xf = x.reshape(t, d)
    h = _rmsnorm(xf, pre_gain(0))
    for i in range(depth):
        j = i // 2
        g_next = pre_gain(i + 1) if i + 1 < depth else None
        if i % 2 == 0:
            z = _in_proj(h, e_w_in, j)
            z3 = z.reshape(b, s, z.shape[1])
            ya = _pool_mixer(z3, e_pool_w[j], e_pool_scale[j])
            yb = _dilated_attention(z3, 2 * pool_width, n_heads)
            w_out, g_post = e_w_out, e_post_norm[j]
        else:
            z = _in_proj(h, o_w_in, j)
            z3 = z.reshape(b, s, z.shape[1])
            ya = _sgu(z, o_sgu_norm_g[j], o_sgu_norm_b[j], o_sgu_w[j], o_sgu_b[j])
            yb = _conv_module(z3, 3 * sgu_width, o_conv_w[j], o_conv_b[j],
                              o_conv_norm_g[j], o_conv_norm_b[j])
            w_out, g_post = o_w_out, o_post_norm[j]
        outs = _out_proj(ya.reshape(t, -1), yb.reshape(t, -1), w_out, j, xf, g_post, g_next)
        if g_next is None:
            xf = outs[0]
        else:
            xf, h = outs
    return xf.reshape(b, s, d)
```

```python
import functools

import jax
import jax.numpy as jnp
import numpy as np
from jax import lax
from jax.experimental import pallas as pl
from jax.experimental.pallas import tpu as pltpu

F32 = jnp.float32
BF16 = jnp.bfloat16

EPS = 1e-6
NEG = -1e30
HEAD_DIM = 128
ROT_DIM = HEAD_DIM // 4
ROT_HALF = ROT_DIM // 2
ROPE_THETA = 500000.0
DILATIONS = (1, 4, 16)
SPAN = 128
ATT_GROUPS = len(DILATIONS)
POOL_WINDOWS = (2, 4, 8, 16)
POOL_CH = 256
SGU_GROUPS = 4
SGU_CH = 256
CHUNK = 128
CONV_K = 31

LANES = 128
VMEM_LIMIT = 56 * 1024 * 1024


def _params(sem, vmem=VMEM_LIMIT):
    return pltpu.CompilerParams(dimension_semantics=sem, vmem_limit_bytes=vmem)


def _silu(x):
    return x * jax.nn.sigmoid(x)


def _rmsnorm_kernel(x_ref, g_ref, o_ref):
    x = x_ref[...]
    ms = jnp.mean(x * x, axis=-1, keepdims=True)
    o_ref[...] = (x * lax.rsqrt(ms + EPS) * g_ref[...]).astype(o_ref.dtype)


def _rmsnorm(x, g, tm=512):
    t, d = x.shape
    return pl.pallas_call(
        _rmsnorm_kernel,
        grid=(t // tm,),
        in_specs=[pl.BlockSpec((tm, d), lambda i: (i, 0)),
                  pl.BlockSpec((1, d), lambda i: (0, 0))],
        out_specs=pl.BlockSpec((tm, d), lambda i: (i, 0)),
        out_shape=jax.ShapeDtypeStruct((t, d), BF16),
        compiler_params=_params(("parallel",)),
        name="rmsnorm",
    )(x, g.reshape(1, d))


CAST_ROWS = 256


def _cast_weight(w_ref, wb_ref):
    def chunk(i, carry):
        rows = pl.ds(pl.multiple_of(i * CAST_ROWS, CAST_ROWS), CAST_ROWS)
        wb_ref[rows, :] = w_ref[rows, :].astype(wb_ref.dtype)
        return carry

    lax.fori_loop(0, w_ref.shape[0] // CAST_ROWS, chunk, 0)


def _in_proj_kernel(a_ref, w_ref, o_ref, wb_ref):
    first = pl.program_id(1) == 0

    @pl.when(first)
    def _():
        acc = None
        for c in range(0, w_ref.shape[0], CAST_ROWS):
            wc = w_ref[c:c + CAST_ROWS, :].astype(BF16)
            wb_ref[c:c + CAST_ROWS, :] = wc
            part = jnp.dot(a_ref[:, c:c + CAST_ROWS], wc, preferred_element_type=F32)
            acc = part if acc is None else acc + part
        o_ref[...] = acc.astype(o_ref.dtype)

    @pl.when(jnp.logical_not(first))
    def _():
        o_ref[...] = jnp.dot(a_ref[...], wb_ref[...],
                             preferred_element_type=F32).astype(o_ref.dtype)


def _in_proj(a, w_all, layer, tm=2048, tn=1024):
    m, k = a.shape
    n = w_all.shape[2]
    return pl.pallas_call(
        _in_proj_kernel,
        grid=(n // tn, m // tm),
        in_specs=[pl.BlockSpec((tm, k), lambda j, i: (i, 0)),
                  pl.BlockSpec((None, k, tn), lambda j, i: (layer, 0, j))],
        out_specs=pl.BlockSpec((tm, tn), lambda j, i: (i, j)),
        out_shape=jax.ShapeDtypeStruct((m, n), BF16),
        scratch_shapes=[pltpu.VMEM((k, tn), BF16)],
        compiler_params=_params(("parallel", "arbitrary")),
        name="in_proj",
    )(a, w_all)


POOL_PAD = 16
POOL_ROWS = 128
POOL_FILL = 256


def _pool_kernel(z_ref, pw_ref, ps_ref, o_ref, xpad_ref, pwb_ref):
    s_len = z_ref.shape[0]
    width = len(POOL_WINDOWS) * POOL_CH
    n_slabs = width // LANES
    slabs_per_group = POOL_CH // LANES
    for g in range(len(POOL_WINDOWS)):
        pwb_ref[g] = pw_ref[g].astype(BF16)
    for c in range(n_slabs):
        xpad_ref[c, 0:POOL_PAD, :] = jnp.zeros((POOL_PAD, LANES), F32)

    def fill(i, carry):
        r = pl.multiple_of(i * POOL_FILL, POOL_FILL)
        for c in range(n_slabs):
            xpad_ref[c, pl.ds(POOL_PAD + r, POOL_FILL), :] = (
                z_ref[pl.ds(r, POOL_FILL), c * LANES:(c + 1) * LANES].astype(F32))
        return carry

    lax.fori_loop(0, s_len // POOL_FILL, fill, 0)

    for r in range(0, s_len, POOL_ROWS):
        for g, w in enumerate(POOL_WINDOWS):
            cs = slice(g * POOL_CH, (g + 1) * POOL_CH)
            xs, accs = [], []
            for c in range(g * slabs_per_group, (g + 1) * slabs_per_group):
                lo = POOL_PAD + r
                xc = xpad_ref[c, lo:lo + POOL_ROWS, :]
                ac = xc
                for j in range(1, w):
                    ac = ac + xpad_ref[c, lo - j:lo - j + POOL_ROWS, :]
                xs.append(xc)
                accs.append(ac)
            x = jnp.concatenate(xs, axis=-1)
            acc = jnp.concatenate(accs, axis=-1)
            if r + 1 >= w:
                cnt = float(w)
            else:
                row = r + lax.broadcasted_iota(jnp.int32, (POOL_ROWS, POOL_CH), 0)
                cnt = jnp.minimum(row + 1, w).astype(F32)
            pooled = acc / cnt - x
            mixed = jnp.dot(pooled.astype(BF16), pwb_ref[g], preferred_element_type=F32)
            gate = z_ref[r:r + POOL_ROWS, width + g * POOL_CH:width + (g + 1) * POOL_CH].astype(F32)
            y = mixed * ps_ref[:, cs] * _silu(gate)
            o_ref[r:r + POOL_ROWS, cs] = y.astype(o_ref.dtype)


def _pool_mixer(z3, pool_w, pool_scale):
    b, s, _ = z3.shape
    width = len(POOL_WINDOWS) * POOL_CH
    return pl.pallas_call(
        _pool_kernel,
        grid=(b,),
        in_specs=[pl.BlockSpec((None, s, 2 * width), lambda i: (i, 0, 0)),
                  pl.BlockSpec(pool_w.shape, lambda i: (0, 0, 0)),
                  pl.BlockSpec((1, width), lambda i: (0, 0))],
        out_specs=pl.BlockSpec((None, s, width), lambda i: (i, 0, 0)),
        out_shape=jax.ShapeDtypeStruct((b, s, width), BF16),
        scratch_shapes=[pltpu.VMEM((width // LANES, POOL_PAD + s, LANES), F32),
                        pltpu.VMEM(pool_w.shape, BF16)],
        compiler_params=_params(("parallel",)),
        name="pool_mixer",
    )(z3, pool_w, pool_scale.reshape(1, width))


ATT_ROWS = 256
ATT_PREP_ROWS = 512
ATT_MAX_BATCH = 16
ATT_INTERLEAVE = 5
assert all(d <= ATT_MAX_BATCH for d in DILATIONS)


def _largest_divisor(n, cap):
    return max(u for u in range(1, cap + 1) if n % u == 0)


def _for_blocks(n, fn):
    per = _largest_divisor(n, ATT_INTERLEAVE)

    def body(it, carry):
        for u in range(per):
            fn(it * per + u)
        return carry

    if n == per:
        body(0, 0)
    else:
        lax.fori_loop(0, n // per, body, 0)


def _aligned(start, align):
    return start if isinstance(start, int) else pl.multiple_of(start, align)


def _attn_kernel(q0, k0, v0, q1, k1, v1, q2, k2, v2, gate_ref, cos_ref, sin_ref,
                 o_ref, qkd_ref, qkv_ref, s_ref, ms_ref, acc_ref, m_ref, l_ref):
    s_len = o_ref.shape[0]
    qkv_in = ((q0, k0, v0), (q1, k1, v1), (q2, k2, v2))

    src = lax.broadcasted_iota(jnp.int32, (HEAD_DIM, HEAD_DIM), 0)
    dst = lax.broadcasted_iota(jnp.int32, (HEAD_DIM, HEAD_DIM), 1)
    perm = (jnp.where((dst < ROT_HALF) & (src == dst + ROT_HALF), -1.0, 0.0)
            + jnp.where((dst >= ROT_HALF) & (dst < ROT_DIM) & (src == dst - ROT_HALF), 1.0, 0.0)
            ).astype(BF16)

    strided = [g for g, d in enumerate(DILATIONS) if d > 1]
    dense = [g for g, d in enumerate(DILATIONS) if d == 1]

    def prep(i, c):
        chunk = pl.ds(pl.multiple_of(i * ATT_PREP_ROWS, ATT_PREP_ROWS), ATT_PREP_ROWS)
        cos, sin = cos_ref[chunk, :], sin_ref[chunk, :]
        for g, d in enumerate(DILATIONS):
            for j in range(2):
                t = qkv_in[g][j][chunk, :]
                partner = jnp.dot(t, perm, preferred_element_type=F32)
                roped = t.astype(F32) * cos + partner * sin
                if d == 1:
                    qkd_ref[2 * dense.index(g) + j, chunk, :] = roped.astype(BF16)
                else:
                    qkv_ref[3 * strided.index(g) + j, chunk, :] = roped
            if d > 1:
                qkv_ref[3 * strided.index(g) + 2, chunk, :] = qkv_in[g][2][chunk, :].astype(F32)
        return c

    lax.fori_loop(0, s_len // ATT_PREP_ROWS, prep, 0)

    qi = lax.broadcasted_iota(jnp.int32, (SPAN, 2 * SPAN), 0)
    kj = lax.broadcasted_iota(jnp.int32, (SPAN, 2 * SPAN), 1)
    band = (kj >= qi) & (kj <= qi + SPAN)
    causal = (lax.broadcasted_iota(jnp.int32, (SPAN, SPAN), 1)
              <= lax.broadcasted_iota(jnp.int32, (SPAN, SPAN), 0))

    def rows(d, start, cnt):
        return pl.ds(_aligned(start, SPAN), cnt) if d == 1 else pl.ds(start, cnt, stride=d)

    def operand(g, d, j, start, cnt):
        if d == 1:
            src = qkv_in[g][2] if j == 2 else qkd_ref.at[2 * dense.index(g) + j]
            return src[rows(d, start, cnt), :]
        return qkv_ref[3 * strided.index(g) + j, rows(d, start, cnt), :].astype(BF16)

    def run_batch(n, n_keys, mask, spec):
        assert n <= ATT_MAX_BATCH

        def blk(j):
            return pl.ds(_aligned(j * SPAN, SPAN), SPAN)

        def scores(j):
            g, d, q_start, k_start = spec(j)
            s = lax.dot_general(operand(g, d, 0, q_start, SPAN), operand(g, d, 1, k_start, n_keys),
                                (((1,), (1,)), ((), ())), preferred_element_type=F32)
            s = jnp.where(mask, s * (HEAD_DIM ** -0.5), NEG)
            s_ref[blk(j), 0:n_keys] = s
            m = jnp.max(s, axis=-1, keepdims=True)
            ms_ref[blk(j), :] = jnp.broadcast_to(m, (SPAN, HEAD_DIM))

        def outputs(j):
            g, d, q_start, k_start = spec(j)
            m = ms_ref[blk(j), :]
            den, ps = None, []
            for c in range(n_keys // HEAD_DIM):
                p = jnp.exp(s_ref[blk(j), c * HEAD_DIM:(c + 1) * HEAD_DIM] - m)
                ps.append(p.astype(BF16))
                den = p if den is None else den + p
            den = jnp.sum(den, axis=-1, keepdims=True)
            m_ref[g, rows(d, q_start, SPAN), :] = m
            l_ref[g, rows(d, q_start, SPAN), :] = jnp.broadcast_to(den, (SPAN, HEAD_DIM))
            pb = ps[0] if len(ps) == 1 else jnp.concatenate(ps, axis=1)
            acc_ref[g, rows(d, q_start, SPAN), :] = jnp.dot(
                pb, operand(g, d, 2, k_start, n_keys), preferred_element_type=F32)

        for stage in (scores, outputs):
            _for_blocks(n, stage)

    few = [(g, d, r, r) for g, d in enumerate(DILATIONS) if d <= ATT_INTERLEAVE for r in range(d)]
    if few:
        for lo in range(0, len(few), ATT_INTERLEAVE):
            part = few[lo:lo + ATT_INTERLEAVE]
            run_batch(len(part), SPAN, causal, lambda j, part=part: part[j])
    for g, d in enumerate(DILATIONS):
        if d > ATT_INTERLEAVE:
            run_batch(d, SPAN, causal, lambda j, g=g, d=d: (g, d, j, j))

    for g, d in enumerate(DILATIONS):
        n_blocks = s_len // (d * SPAN)
        stride_blk = d * SPAN

        def spec(j, g=g, d=d, stride_blk=stride_blk):
            if d == 1:
                return g, d, (1 + j) * SPAN, j * SPAN
            if isinstance(j, int):
                r, i = j % d, 1 + j // d
            else:
                r, i = lax.rem(j, d), 1 + lax.div(j, d)
            q_start = r + i * stride_blk
            return g, d, q_start, q_start - stride_blk

        if n_blocks > 1:
            run_batch(d * (n_blocks - 1), 2 * SPAN, band, spec)

    def merge(i, c):
        r = pl.multiple_of(i * ATT_ROWS, ATT_ROWS)
        rows = pl.ds(r, ATT_ROWS)
        ms = [m_ref[g, rows, :] for g in range(ATT_GROUPS)]
        m = functools.reduce(jnp.maximum, ms)
        num = den = None
        for g in range(ATT_GROUPS):
            e = jnp.exp(ms[g] - m)
            ng, dg = e * acc_ref[g, rows, :], e * l_ref[g, rows, :]
            num, den = (ng, dg) if num is None else (num + ng, den + dg)
        gate = gate_ref[rows, :].astype(F32)
        o_ref[rows, :] = (num / den * _silu(gate)).astype(o_ref.dtype)
        return c

    lax.fori_loop(0, s_len // ATT_ROWS, merge, 0)


def _rope_tables(s_len):
    pos = jnp.arange(s_len, dtype=F32)
    inv_freq = jnp.power(ROPE_THETA, -jnp.arange(0, ROT_DIM, 2, dtype=F32) / ROT_DIM)
    ang = pos[:, None] * inv_freq[None, :]
    cos, sin = jnp.cos(ang), jnp.sin(ang)
    rest = jnp.zeros((s_len, HEAD_DIM - ROT_DIM), F32)
    return (jnp.concatenate([cos, cos, rest + 1.0], axis=-1),
            jnp.concatenate([sin, sin, rest], axis=-1))


def _dilated_attention(z3, q_col, n_heads):
    b, s, _ = z3.shape
    per = ATT_GROUPS * n_heads
    qb = q_col // HEAD_DIM
    cos_t, sin_t = _rope_tables(s)

    def head_spec(blk0):
        return pl.BlockSpec((None, s, HEAD_DIM), lambda i, h, blk0=blk0: (i, 0, blk0 + h))

    in_specs = []
    for g in range(ATT_GROUPS):
        for j in range(3):
            in_specs.append(head_spec(qb + j * per + g * n_heads))
    in_specs.append(head_spec(qb + 3 * per))
    tab_spec = pl.BlockSpec((s, HEAD_DIM), lambda i, h: (0, 0))
    in_specs += [tab_spec, tab_spec]
    return pl.pallas_call(
        _attn_kernel,
        grid=(b, n_heads),
        in_specs=in_specs,
        out_specs=pl.BlockSpec((None, s, HEAD_DIM), lambda i, h: (i, 0, h)),
        out_shape=jax.ShapeDtypeStruct((b, s, n_heads * HEAD_DIM), BF16),
        scratch_shapes=[pltpu.VMEM((2 * sum(d == 1 for d in DILATIONS), s, HEAD_DIM), BF16),
                        pltpu.VMEM((3 * sum(d > 1 for d in DILATIONS), s, HEAD_DIM), F32),
                        pltpu.VMEM((ATT_MAX_BATCH * SPAN, 2 * SPAN), F32),
                        pltpu.VMEM((ATT_MAX_BATCH * SPAN, HEAD_DIM), F32),
                        pltpu.VMEM((ATT_GROUPS, s, HEAD_DIM), F32),
                        pltpu.VMEM((ATT_GROUPS, s, HEAD_DIM), F32),
                        pltpu.VMEM((ATT_GROUPS, s, HEAD_DIM), F32)],
        compiler_params=_params(("parallel", "parallel")),
        name="dilated_attention",
    )(*([z3] * 10), cos_t, sin_t)


OUT_SUB_ROWS = 128


def _out_proj_kernel(ya_ref, yb_ref, w_ref, x_ref, gpost_ref, *rest, with_next):
    if with_next:
        gnext_ref, xo_ref, ho_ref, wb_ref = rest
    else:
        xo_ref, wb_ref = rest

    @pl.when(pl.program_id(0) == 0)
    def _():
        _cast_weight(w_ref, wb_ref)

    half = ya_ref.shape[1]
    for r in range(0, ya_ref.shape[0], OUT_SUB_ROWS):
        rows = slice(r, r + OUT_SUB_ROWS)
        y = jnp.dot(ya_ref[rows, :], wb_ref[0:half, :], preferred_element_type=F32)
        y = y + jnp.dot(yb_ref[rows, :], wb_ref[half:2 * half, :], preferred_element_type=F32)
        yn = y * lax.rsqrt(jnp.mean(y * y, axis=-1, keepdims=True) + EPS) * gpost_ref[...]
        xn = x_ref[rows, :] + yn
        xo_ref[rows, :] = xn
        if with_next:
            hn = xn * lax.rsqrt(jnp.mean(xn * xn, axis=-1, keepdims=True) + EPS) * gnext_ref[...]
            ho_ref[rows, :] = hn.astype(ho_ref.dtype)


def _out_proj(ya, yb, w_all, layer, x, g_post, g_next=None, tm=512):
    t, half = ya.shape
    _, k, d = w_all.shape
    with_next = g_next is not None
    row = lambda i: (i, 0)
    fixed = lambda i: (0, 0)
    in_specs = [pl.BlockSpec((tm, half), row), pl.BlockSpec((tm, half), row),
                pl.BlockSpec((None, k, d), lambda i: (layer, 0, 0), pipeline_mode=pl.Buffered(1)),
                pl.BlockSpec((tm, d), row),
                pl.BlockSpec((1, d), fixed)]
    args = [ya, yb, w_all, x, g_post.reshape(1, d)]
    out_specs = [pl.BlockSpec((tm, d), row)]
    out_shape = [jax.ShapeDtypeStruct((t, d), F32)]
    if with_next:
        in_specs.append(pl.BlockSpec((1, d), fixed))
        args.append(g_next.reshape(1, d))
        out_specs.append(pl.BlockSpec((tm, d), row))
        out_shape.append(jax.ShapeDtypeStruct((t, d), BF16))
    return pl.pallas_call(
        functools.partial(_out_proj_kernel, with_next=with_next),
        grid=(t // tm,),
        in_specs=in_specs,
        out_specs=out_specs,
        out_shape=out_shape,
        scratch_shapes=[pltpu.VMEM((k, d), BF16)],
        compiler_params=_params(("arbitrary",)),
        name="out_proj",
    )(*args)


def _sgu_kernel(u_ref, v_ref, cg_ref, g_ref, b_ref, ws_ref, bias_ref, o_ref):
    tm = u_ref.shape[0]
    ii = lax.broadcasted_iota(jnp.int32, (CHUNK, CHUNK), 0)
    jj = lax.broadcasted_iota(jnp.int32, (CHUNK, CHUNK), 1)
    tril = jj <= ii
    for c in range(tm // CHUNK):
        rows = slice(c * CHUNK, (c + 1) * CHUNK)
        v = v_ref[rows, :].astype(F32)
        mu = jnp.mean(v, axis=-1, keepdims=True)
        vc = v - mu
        var = jnp.mean(vc * vc, axis=-1, keepdims=True)
        vn = (vc * lax.rsqrt(var + EPS) * g_ref[...] + b_ref[...]).astype(BF16)
        for h in range(SGU_GROUPS):
            cs = slice(h * SGU_CH, (h + 1) * SGU_CH)
            wm = jnp.where(tril, ws_ref[h], 0.0).astype(BF16)
            sg = jnp.dot(wm, vn[:, cs], preferred_element_type=F32) + bias_ref[:, cs]
            y = u_ref[rows, cs].astype(F32) * sg * _silu(cg_ref[rows, cs].astype(F32))
            o_ref[rows, cs] = y.astype(o_ref.dtype)


def _sgu(z, sgu_g, sgu_b, sgu_w, sgu_bias, tm=512):
    t = z.shape[0]
    width = SGU_GROUPS * SGU_CH
    bias_full = jnp.repeat(sgu_bias.T, SGU_CH, axis=1)
    fixed2 = lambda i: (0, 0)
    return pl.pallas_call(
        _sgu_kernel,
        grid=(t // tm,),
        in_specs=[pl.BlockSpec((tm, width), lambda i: (i, 0)),
                  pl.BlockSpec((tm, width), lambda i: (i, 1)),
                  pl.BlockSpec((tm, width), lambda i: (i, 2)),
                  pl.BlockSpec((1, width), fixed2), pl.BlockSpec((1, width), fixed2),
                  pl.BlockSpec(sgu_w.shape, lambda i: (0, 0, 0)),
                  pl.BlockSpec((CHUNK, width), fixed2)],
        out_specs=pl.BlockSpec((tm, width), lambda i: (i, 0)),
        out_shape=jax.ShapeDtypeStruct((t, width), BF16),
        compiler_params=_params(("parallel",)),
        name="sgu",
    )(z, z, z, sgu_g.reshape(1, width), sgu_b.reshape(1, width), sgu_w, bias_full)


CONV_PAD = 32
CONV_ROWS = 128
CONV_FILL = 128


def _conv_kernel(dv_ref, dg_ref, w_ref, cb_ref, o_ref, dpad_ref):
    s_len = dv_ref.shape[0]
    dpad_ref[0:CONV_PAD, :] = jnp.zeros((CONV_PAD, LANES), F32)

    def fill(i, carry):
        r = pl.multiple_of(i * CONV_FILL, CONV_FILL)
        rows = pl.ds(r, CONV_FILL)
        d = dv_ref[rows, :].astype(F32) * jax.nn.sigmoid(dg_ref[rows, :].astype(F32))
        dpad_ref[pl.ds(CONV_PAD + r, CONV_FILL), :] = d
        return carry

    lax.fori_loop(0, s_len // CONV_FILL, fill, 0)

    lead = CONV_PAD - (CONV_K - 1)
    taps = [w_ref[k:k + 1, :] for k in range(CONV_K)]
    for r in range(0, s_len, CONV_ROWS):
        acc = dpad_ref[r + lead:r + lead + CONV_ROWS, :] * taps[0]
        for k in range(1, CONV_K):
            acc = acc + dpad_ref[r + lead + k:r + lead + k + CONV_ROWS, :] * taps[k]
        o_ref[r:r + CONV_ROWS, :] = acc + cb_ref[...]


def _conv_norm_kernel(c_ref, gate_ref, ng_ref, nb_ref, o_ref):
    acc = c_ref[...]
    mu = jnp.mean(acc, axis=-1, keepdims=True)
    ac = acc - mu
    var = jnp.mean(ac * ac, axis=-1, keepdims=True)
    y = _silu(ac * lax.rsqrt(var + EPS) * ng_ref[...] + nb_ref[...])
    o_ref[...] = (y * _silu(gate_ref[...].astype(F32))).astype(o_ref.dtype)


def _conv_module(z3, col, conv_w, conv_b, cn_g, cn_b, tm=512):
    b, s, cols = z3.shape
    ch = conv_w.shape[1]
    n_slabs = ch // LANES
    blk = col // LANES
    vec = lambda a: a.reshape(1, ch)
    conv = pl.pallas_call(
        _conv_kernel,
        grid=(b, n_slabs),
        in_specs=[pl.BlockSpec((None, s, LANES), lambda i, c: (i, 0, blk + c)),
                  pl.BlockSpec((None, s, LANES), lambda i, c: (i, 0, blk + n_slabs + c)),
                  pl.BlockSpec((CONV_K, LANES), lambda i, c: (0, c)),
                  pl.BlockSpec((1, LANES), lambda i, c: (0, c))],
        out_specs=pl.BlockSpec((None, s, LANES), lambda i, c: (i, 0, c)),
        out_shape=jax.ShapeDtypeStruct((b, s, ch), F32),
        scratch_shapes=[pltpu.VMEM((CONV_PAD + s, LANES), F32)],
        compiler_params=_params(("parallel", "parallel")),
        name="conv_taps",
    )(z3, z3, conv_w, vec(conv_b))
    t = b * s
    fixed2 = lambda i: (0, 0)
    gate_blk = col // ch + 2
    return pl.pallas_call(
        _conv_norm_kernel,
        grid=(t // tm,),
        in_specs=[pl.BlockSpec((tm, ch), lambda i: (i, 0)),
                  pl.BlockSpec((tm, ch), lambda i: (i, gate_blk)),
                  pl.BlockSpec((1, ch), fixed2), pl.BlockSpec((1, ch), fixed2)],
        out_specs=pl.BlockSpec((tm, ch), lambda i: (i, 0)),
        out_shape=jax.ShapeDtypeStruct((t, ch), BF16),
        compiler_params=_params(("parallel",)),
        name="conv_norm",
    )(conv.reshape(t, ch), z3.reshape(t, cols), vec(cn_g), vec(cn_b))


def kernel(x, e_pre_norm, e_w_in, e_pool_w, e_pool_scale, e_w_out, e_post_norm, o_pre_norm, o_w_in, o_sgu_norm_g, o_sgu_norm_b, o_sgu_w, o_sgu_b, o_conv_w, o_conv_b, o_conv_norm_g, o_conv_norm_b, o_w_out, o_post_norm):
    b, s, d = x.shape
    t = b * s
    n_even, n_odd = e_w_in.shape[0], o_w_in.shape[0]
    depth = n_even + n_odd
    pool_width = len(POOL_WINDOWS) * POOL_CH
    n_heads = (e_w_out.shape[1] - pool_width) // HEAD_DIM
    sgu_width = SGU_GROUPS * SGU_CH

    def pre_gain(i):
        return e_pre_norm[i // 2] if i % 2 == 0 else o_pre_norm[i // 2]

    xf = x.reshape(t, d)
    h = _rmsnorm(xf, pre_gain(0))
    for i in range(depth):
        j = i // 2
        g_next = pre_gain(i + 1) if i + 1 < depth else None
        if i % 2 == 0:
            z = _in_proj(h, e_w_in, j)
            z3 = z.reshape(b, s, z.shape[1])
            ya = _pool_mixer(z3, e_pool_w[j], e_pool_scale[j])
            yb = _dilated_attention(z3, 2 * pool_width, n_heads)
            w_out, g_post = e_w_out, e_post_norm[j]
        else:
            z = _in_proj(h, o_w_in, j)
            z3 = z.reshape(b, s, z.shape[1])
            ya = _sgu(z, o_sgu_norm_g[j], o_sgu_norm_b[j], o_sgu_w[j], o_sgu_b[j])
            yb = _conv_module(z3, 3 * sgu_width, o_conv_w[j], o_conv_b[j],
                              o_conv_norm_g[j], o_conv_norm_b[j])
            w_out, g_post = o_w_out, o_post_norm[j]
        outs = _out_proj(ya.reshape(t, -1), yb.reshape(t, -1), w_out, j, xf, g_post, g_next)
        if g_next is None:
            xf = outs[0]
        else:
            xf, h = outs
    return xf.reshape(b, s, d)
```

```python
import functools

import jax
import jax.numpy as jnp
import numpy as np
from jax import lax
from jax.experimental import pallas as pl
from jax.experimental.pallas import tpu as pltpu

F32 = jnp.float32
BF16 = jnp.bfloat16

EPS = 1e-6
NEG = -1e30
HEAD_DIM = 128
ROT_DIM = HEAD_DIM // 4
ROT_HALF = ROT_DIM // 2
ROPE_THETA = 500000.0
DILATIONS = (1, 4, 16)
SPAN = 128
ATT_GROUPS = len(DILATIONS)
POOL_WINDOWS = (2, 4, 8, 16)
POOL_CH = 256
SGU_GROUPS = 4
SGU_CH = 256
CHUNK = 128
CONV_K = 31

LANES = 128
VMEM_LIMIT = 56 * 1024 * 1024


def _params(sem, vmem=VMEM_LIMIT):
    return pltpu.CompilerParams(dimension_semantics=sem, vmem_limit_bytes=vmem)


def _silu(x):
    return x * jax.nn.sigmoid(x)


def _rmsnorm_kernel(x_ref, g_ref, o_ref):
    x = x_ref[...]
    ms = jnp.mean(x * x, axis=-1, keepdims=True)
    o_ref[...] = (x * lax.rsqrt(ms + EPS) * g_ref[...]).astype(o_ref.dtype)


def _rmsnorm(x, g, tm=512):
    t, d = x.shape
    return pl.pallas_call(
        _rmsnorm_kernel,
        grid=(t // tm,),
        in_specs=[pl.BlockSpec((tm, d), lambda i: (i, 0)),
                  pl.BlockSpec((1, d), lambda i: (0, 0))],
        out_specs=pl.BlockSpec((tm, d), lambda i: (i, 0)),
        out_shape=jax.ShapeDtypeStruct((t, d), BF16),
        compiler_params=_params(("parallel",)),
        name="rmsnorm",
    )(x, g.reshape(1, d))


CAST_ROWS = 256


def _cast_weight(w_ref, wb_ref):
    def chunk(i, carry):
        rows = pl.ds(pl.multiple_of(i * CAST_ROWS, CAST_ROWS), CAST_ROWS)
        wb_ref[rows, :] = w_ref[rows, :].astype(wb_ref.dtype)
        return carry

    lax.fori_loop(0, w_ref.shape[0] // CAST_ROWS, chunk, 0)


def _in_proj_kernel(a_ref, w_ref, o_ref, wb_ref):
    first = pl.program_id(1) == 0

    @pl.when(first)
    def _():
        acc = None
        for c in range(0, w_ref.shape[0], CAST_ROWS):
            wc = w_ref[c:c + CAST_ROWS, :].astype(BF16)
            wb_ref[c:c + CAST_ROWS, :] = wc
            part = jnp.dot(a_ref[:, c:c + CAST_ROWS], wc, preferred_element_type=F32)
            acc = part if acc is None else acc + part
        o_ref[...] = acc.astype(o_ref.dtype)

    @pl.when(jnp.logical_not(first))
    def _():
        o_ref[...] = jnp.dot(a_ref[...], wb_ref[...],
                             preferred_element_type=F32).astype(o_ref.dtype)


def _in_proj(a, w_all, layer, tm=2048, tn=1024):
    m, k = a.shape
    n = w_all.shape[2]
    return pl.pallas_call(
        _in_proj_kernel,
        grid=(n // tn, m // tm),
        in_specs=[pl.BlockSpec((tm, k), lambda j, i: (i, 0)),
                  pl.BlockSpec((None, k, tn), lambda j, i: (layer, 0, j))],
        out_specs=pl.BlockSpec((tm, tn), lambda j, i: (i, j)),
        out_shape=jax.ShapeDtypeStruct((m, n), BF16),
        scratch_shapes=[pltpu.VMEM((k, tn), BF16)],
        compiler_params=_params(("parallel", "arbitrary")),
        name="in_proj",
    )(a, w_all)


POOL_PAD = 16
POOL_ROWS = 128
POOL_FILL = 256


def _pool_kernel(z_ref, pw_ref, ps_ref, o_ref, xpad_ref, pwb_ref):
    s_len = z_ref.shape[0]
    width = len(POOL_WINDOWS) * POOL_CH
    n_slabs = width // LANES
    slabs_per_group = POOL_CH // LANES
    for g in range(len(POOL_WINDOWS)):
        pwb_ref[g] = pw_ref[g].astype(BF16)
    for c in range(n_slabs):
        xpad_ref[c, 0:POOL_PAD, :] = jnp.zeros((POOL_PAD, LANES), F32)

    def fill(i, carry):
        r = pl.multiple_of(i * POOL_FILL, POOL_FILL)
        for c in range(n_slabs):
            xpad_ref[c, pl.ds(POOL_PAD + r, POOL_FILL), :] = (
                z_ref[pl.ds(r, POOL_FILL), c * LANES:(c + 1) * LANES].astype(F32))
        return carry

    lax.fori_loop(0, s_len // POOL_FILL, fill, 0)

    for r in range(0, s_len, POOL_ROWS):
        for g, w in enumerate(POOL_WINDOWS):
            cs = slice(g * POOL_CH, (g + 1) * POOL_CH)
            xs, accs = [], []
            for c in range(g * slabs_per_group, (g + 1) * slabs_per_group):
                lo = POOL_PAD + r
                xc = xpad_ref[c, lo:lo + POOL_ROWS, :]
                ac = xc
                for j in range(1, w):
                    ac = ac + xpad_ref[c, lo - j:lo - j + POOL_ROWS, :]
                xs.append(xc)
                accs.append(ac)
            x = jnp.concatenate(xs, axis=-1)
            acc = jnp.concatenate(accs, axis=-1)
            if r + 1 >= w:
                cnt = float(w)
            else:
                row = r + lax.broadcasted_iota(jnp.int32, (POOL_ROWS, POOL_CH), 0)
                cnt = jnp.minimum(row + 1, w).astype(F32)
            pooled = acc / cnt - x
            mixed = jnp.dot(pooled.astype(BF16), pwb_ref[g], preferred_element_type=F32)
            gate = z_ref[r:r + POOL_ROWS, width + g * POOL_CH:width + (g + 1) * POOL_CH].astype(F32)
            y = mixed * ps_ref[:, cs] * _silu(gate)
            o_ref[r:r + POOL_ROWS, cs] = y.astype(o_ref.dtype)


def _pool_mixer(z3, pool_w, pool_scale):
    b, s, _ = z3.shape
    width = len(POOL_WINDOWS) * POOL_CH
    return pl.pallas_call(
        _pool_kernel,
        grid=(b,),
        in_specs=[pl.BlockSpec((None, s, 2 * width), lambda i: (i, 0, 0)),
                  pl.BlockSpec(pool_w.shape, lambda i: (0, 0, 0)),
                  pl.BlockSpec((1, width), lambda i: (0, 0))],
        out_specs=pl.BlockSpec((None, s, width), lambda i: (i, 0, 0)),
        out_shape=jax.ShapeDtypeStruct((b, s, width), BF16),
        scratch_shapes=[pltpu.VMEM((width // LANES, POOL_PAD + s, LANES), F32),
                        pltpu.VMEM(pool_w.shape, BF16)],
        compiler_params=_params(("parallel",)),
        name="pool_mixer",
    )(z3, pool_w, pool_scale.reshape(1, width))


ATT_ROWS = 1024
ATT_PREP_ROWS = 2048
ATT_MAX_BATCH = 16
ATT_INTERLEAVE = 16
assert all(d <= ATT_MAX_BATCH for d in DILATIONS)


def _largest_divisor(n, cap):
    return max(u for u in range(1, cap + 1) if n % u == 0)


def _for_blocks(n, fn):
    per = _largest_divisor(n, ATT_INTERLEAVE)

    def body(it, carry):
        for u in range(per):
            fn(it * per + u)
        return carry

    if n == per:
        body(0, 0)
    else:
        lax.fori_loop(0, n // per, body, 0)


def _for_chunks(n, body):
    if n == 1:
        body(0, 0)
    else:
        lax.fori_loop(0, n, body, 0)


def _aligned(start, align):
    return start if isinstance(start, int) else pl.multiple_of(start, align)


def _attn_kernel(q0, k0, v0, q1, k1, v1, q2, k2, v2, gate_ref, cos_ref, sin_ref,
                 o_ref, qkd_ref, qkv_ref, s_ref, ms_ref, acc_ref, m_ref, l_ref):
    s_len = o_ref.shape[0]
    qkv_in = ((q0, k0, v0), (q1, k1, v1), (q2, k2, v2))

    src = lax.broadcasted_iota(jnp.int32, (HEAD_DIM, HEAD_DIM), 0)
    dst = lax.broadcasted_iota(jnp.int32, (HEAD_DIM, HEAD_DIM), 1)
    perm = (jnp.where((dst < ROT_HALF) & (src == dst + ROT_HALF), -1.0, 0.0)
            + jnp.where((dst >= ROT_HALF) & (dst < ROT_DIM) & (src == dst - ROT_HALF), 1.0, 0.0)
            ).astype(BF16)

    strided = [g for g, d in enumerate(DILATIONS) if d > 1]
    dense = [g for g, d in enumerate(DILATIONS) if d == 1]

    prep_rows, merge_rows = min(ATT_PREP_ROWS, s_len), min(ATT_ROWS, s_len)

    def prep(i, c):
        chunk = pl.ds(_aligned(i * prep_rows, prep_rows), prep_rows)
        cos, sin = cos_ref[chunk, :], sin_ref[chunk, :]
        for g, d in enumerate(DILATIONS):
            for j in range(2):
                t = qkv_in[g][j][chunk, :]
                partner = jnp.dot(t, perm, preferred_element_type=F32)
                roped = t.astype(F32) * cos + partner * sin
                if d == 1:
                    qkd_ref[2 * dense.index(g) + j, chunk, :] = roped.astype(BF16)
                else:
                    qkv_ref[3 * strided.index(g) + j, chunk, :] = roped
            if d > 1:
                qkv_ref[3 * strided.index(g) + 2, chunk, :] = qkv_in[g][2][chunk, :].astype(F32)
        return c

    _for_chunks(s_len // prep_rows, prep)

    qi = lax.broadcasted_iota(jnp.int32, (SPAN, 2 * SPAN), 0)
    kj = lax.broadcasted_iota(jnp.int32, (SPAN, 2 * SPAN), 1)
    band = (kj >= qi) & (kj <= qi + SPAN)
    causal = (lax.broadcasted_iota(jnp.int32, (SPAN, SPAN), 1)
              <= lax.broadcasted_iota(jnp.int32, (SPAN, SPAN), 0))

    def rows(d, start, cnt):
        return pl.ds(_aligned(start, SPAN), cnt) if d == 1 else pl.ds(start, cnt, stride=d)

    def operand(g, d, j, start, cnt):
        if d == 1:
            src = qkv_in[g][2] if j == 2 else qkd_ref.at[2 * dense.index(g) + j]
            return src[rows(d, start, cnt), :]
        return qkv_ref[3 * strided.index(g) + j, rows(d, start, cnt), :].astype(BF16)

    def run_batch(n, n_keys, mask, spec):
        assert n <= ATT_MAX_BATCH

        def blk(j):
            return pl.ds(_aligned(j * SPAN, SPAN), SPAN)

        def scores(j):
            g, d, q_start, k_start = spec(j)
            s = lax.dot_general(operand(g, d, 0, q_start, SPAN), operand(g, d, 1, k_start, n_keys),
                                (((1,), (1,)), ((), ())), preferred_element_type=F32)
            s = jnp.where(mask, s * (HEAD_DIM ** -0.5), NEG)
            s_ref[blk(j), 0:n_keys] = s
            m = jnp.max(s, axis=-1, keepdims=True)
            ms_ref[blk(j), :] = jnp.broadcast_to(m, (SPAN, HEAD_DIM))

        def outputs(j):
            g, d, q_start, k_start = spec(j)
            m = ms_ref[blk(j), :]
            den, ps = None, []
            for c in range(n_keys // HEAD_DIM):
                p = jnp.exp(s_ref[blk(j), c * HEAD_DIM:(c + 1) * HEAD_DIM] - m)
                ps.append(p.astype(BF16))
                den = p if den is None else den + p
            den = jnp.sum(den, axis=-1, keepdims=True)
            m_ref[g, rows(d, q_start, SPAN), :] = m
            l_ref[g, rows(d, q_start, SPAN), :] = jnp.broadcast_to(den, (SPAN, HEAD_DIM))
            pb = ps[0] if len(ps) == 1 else jnp.concatenate(ps, axis=1)
            acc_ref[g, rows(d, q_start, SPAN), :] = jnp.dot(
                pb, operand(g, d, 2, k_start, n_keys), preferred_element_type=F32)

        for stage in (scores, outputs):
            _for_blocks(n, stage)

    few = [(g, d, r, r) for g, d in enumerate(DILATIONS) if d <= ATT_INTERLEAVE for r in range(d)]
    if few:
        for lo in range(0, len(few), ATT_INTERLEAVE):
            part = few[lo:lo + ATT_INTERLEAVE]
            run_batch(len(part), SPAN, causal, lambda j, part=part: part[j])
    for g, d in enumerate(DILATIONS):
        if d > ATT_INTERLEAVE:
            run_batch(d, SPAN, causal, lambda j, g=g, d=d: (g, d, j, j))

    for g, d in enumerate(DILATIONS):
        n_blocks = s_len // (d * SPAN)
        stride_blk = d * SPAN

        def spec(j, g=g, d=d, stride_blk=stride_blk):
            if d == 1:
                return g, d, (1 + j) * SPAN, j * SPAN
            if isinstance(j, int):
                r, i = j % d, 1 + j // d
            else:
                r, i = lax.rem(j, d), 1 + lax.div(j, d)
            q_start = r + i * stride_blk
            return g, d, q_start, q_start - stride_blk

        if n_blocks > 1:
            run_batch(d * (n_blocks - 1), 2 * SPAN, band, spec)

    def merge(i, c):
        rows = pl.ds(_aligned(i * merge_rows, merge_rows), merge_rows)
        ms = [m_ref[g, rows, :] for g in range(ATT_GROUPS)]
        m = functools.reduce(jnp.maximum, ms)
        num = den = None
        for g in range(ATT_GROUPS):
            e = jnp.exp(ms[g] - m)
            ng, dg = e * acc_ref[g, rows, :], e * l_ref[g, rows, :]
            num, den = (ng, dg) if num is None else (num + ng, den + dg)
        gate = gate_ref[rows, :].astype(F32)
        o_ref[rows, :] = (num / den * _silu(gate)).astype(o_ref.dtype)
        return c

    _for_chunks(s_len // merge_rows, merge)


def _rope_tables(s_len):
    pos = jnp.arange(s_len, dtype=F32)
    inv_freq = jnp.power(ROPE_THETA, -jnp.arange(0, ROT_DIM, 2, dtype=F32) / ROT_DIM)
    ang = pos[:, None] * inv_freq[None, :]
    cos, sin = jnp.cos(ang), jnp.sin(ang)
    rest = jnp.zeros((s_len, HEAD_DIM - ROT_DIM), F32)
    return (jnp.concatenate([cos, cos, rest + 1.0], axis=-1),
            jnp.concatenate([sin, sin, rest], axis=-1))


def _dilated_attention(z3, q_col, n_heads):
    b, s, _ = z3.shape
    per = ATT_GROUPS * n_heads
    qb = q_col // HEAD_DIM
    cos_t, sin_t = _rope_tables(s)

    def head_spec(blk0):
        return pl.BlockSpec((None, s, HEAD_DIM), lambda i, h, blk0=blk0: (i, 0, blk0 + h))

    in_specs = []
    for g in range(ATT_GROUPS):
        for j in range(3):
            in_specs.append(head_spec(qb + j * per + g * n_heads))
    in_specs.append(head_spec(qb + 3 * per))
    tab_spec = pl.BlockSpec((s, HEAD_DIM), lambda i, h: (0, 0))
    in_specs += [tab_spec, tab_spec]
    return pl.pallas_call(
        _attn_kernel,
        grid=(b, n_heads),
        in_specs=in_specs,
        out_specs=pl.BlockSpec((None, s, HEAD_DIM), lambda i, h: (i, 0, h)),
        out_shape=jax.ShapeDtypeStruct((b, s, n_heads * HEAD_DIM), BF16),
        scratch_shapes=[pltpu.VMEM((2 * sum(d == 1 for d in DILATIONS), s, HEAD_DIM), BF16),
                        pltpu.VMEM((3 * sum(d > 1 for d in DILATIONS), s, HEAD_DIM), F32),
                        pltpu.VMEM((ATT_MAX_BATCH * SPAN, 2 * SPAN), F32),
                        pltpu.VMEM((ATT_MAX_BATCH * SPAN, HEAD_DIM), F32),
                        pltpu.VMEM((ATT_GROUPS, s, HEAD_DIM), F32),
                        pltpu.VMEM((ATT_GROUPS, s, HEAD_DIM), F32),
                        pltpu.VMEM((ATT_GROUPS, s, HEAD_DIM), F32)],
        compiler_params=_params(("parallel", "parallel")),
        name="dilated_attention",
    )(*([z3] * 10), cos_t, sin_t)


OUT_SUB_ROWS = 128


def _out_proj_kernel(ya_ref, yb_ref, w_ref, x_ref, gpost_ref, *rest, with_next):
    if with_next:
        gnext_ref, xo_ref, ho_ref, wb_ref = rest
    else:
        xo_ref, wb_ref = rest

    @pl.when(pl.program_id(0) == 0)
    def _():
        _cast_weight(w_ref, wb_ref)

    half = ya_ref.shape[1]
    for r in range(0, ya_ref.shape[0], OUT_SUB_ROWS):
        rows = slice(r, r + OUT_SUB_ROWS)
        y = jnp.dot(ya_ref[rows, :], wb_ref[0:half, :], preferred_element_type=F32)
        y = y + jnp.dot(yb_ref[rows, :], wb_ref[half:2 * half, :], preferred_element_type=F32)
        yn = y * lax.rsqrt(jnp.mean(y * y, axis=-1, keepdims=True) + EPS) * gpost_ref[...]
        xn = x_ref[rows, :] + yn
        xo_ref[rows, :] = xn
        if with_next:
            hn = xn * lax.rsqrt(jnp.mean(xn * xn, axis=-1, keepdims=True) + EPS) * gnext_ref[...]
            ho_ref[rows, :] = hn.astype(ho_ref.dtype)


def _out_proj(ya, yb, w_all, layer, x, g_post, g_next=None, tm=512):
    t, half = ya.shape
    _, k, d = w_all.shape
    with_next = g_next is not None
    row = lambda i: (i, 0)
    fixed = lambda i: (0, 0)
    in_specs = [pl.BlockSpec((tm, half), row), pl.BlockSpec((tm, half), row),
                pl.BlockSpec((None, k, d), lambda i: (layer, 0, 0), pipeline_mode=pl.Buffered(1)),
                pl.BlockSpec((tm, d), row),
                pl.BlockSpec((1, d), fixed)]
    args = [ya, yb, w_all, x, g_post.reshape(1, d)]
    out_specs = [pl.BlockSpec((tm, d), row)]
    out_shape = [jax.ShapeDtypeStruct((t, d), F32)]
    if with_next:
        in_specs.append(pl.BlockSpec((1, d), fixed))
        args.append(g_next.reshape(1, d))
        out_specs.append(pl.BlockSpec((tm, d), row))
        out_shape.append(jax.ShapeDtypeStruct((t, d), BF16))
    return pl.pallas_call(
        functools.partial(_out_proj_kernel, with_next=with_next),
        grid=(t // tm,),
        in_specs=in_specs,
        out_specs=out_specs,
        out_shape=out_shape,
        scratch_shapes=[pltpu.VMEM((k, d), BF16)],
        compiler_params=_params(("arbitrary",)),
        name="out_proj",
    )(*args)


def _sgu_kernel(u_ref, v_ref, cg_ref, g_ref, b_ref, ws_ref, bias_ref, o_ref):
    tm = u_ref.shape[0]
    ii = lax.broadcasted_iota(jnp.int32, (CHUNK, CHUNK), 0)
    jj = lax.broadcasted_iota(jnp.int32, (CHUNK, CHUNK), 1)
    tril = jj <= ii
    for c in range(tm // CHUNK):
        rows = slice(c * CHUNK, (c + 1) * CHUNK)
        v = v_ref[rows, :].astype(F32)
        mu = jnp.mean(v, axis=-1, keepdims=True)
        vc = v - mu
        var = jnp.mean(vc * vc, axis=-1, keepdims=True)
        vn = (vc * lax.rsqrt(var + EPS) * g_ref[...] + b_ref[...]).astype(BF16)
        for h in range(SGU_GROUPS):
            cs = slice(h * SGU_CH, (h + 1) * SGU_CH)
            wm = jnp.where(tril, ws_ref[h], 0.0).astype(BF16)
            sg = jnp.dot(wm, vn[:, cs], preferred_element_type=F32) + bias_ref[:, cs]
            y = u_ref[rows, cs].astype(F32) * sg * _silu(cg_ref[rows, cs].astype(F32))
            o_ref[rows, cs] = y.astype(o_ref.dtype)


def _sgu(z, sgu_g, sgu_b, sgu_w, sgu_bias, tm=512):
    t = z.shape[0]
    width = SGU_GROUPS * SGU_CH
    bias_full = jnp.repeat(sgu_bias.T, SGU_CH, axis=1)
    fixed2 = lambda i: (0, 0)
    return pl.pallas_call(
        _sgu_kernel,
        grid=(t // tm,),
        in_specs=[pl.BlockSpec((tm, width), lambda i: (i, 0)),
                  pl.BlockSpec((tm, width), lambda i: (i, 1)),
                  pl.BlockSpec((tm, width), lambda i: (i, 2)),
                  pl.BlockSpec((1, width), fixed2), pl.BlockSpec((1, width), fixed2),
                  pl.BlockSpec(sgu_w.shape, lambda i: (0, 0, 0)),
                  pl.BlockSpec((CHUNK, width), fixed2)],
        out_specs=pl.BlockSpec((tm, width), lambda i: (i, 0)),
        out_shape=jax.ShapeDtypeStruct((t, width), BF16),
        compiler_params=_params(("parallel",)),
        name="sgu",
    )(z, z, z, sgu_g.reshape(1, width), sgu_b.reshape(1, width), sgu_w, bias_full)


CONV_PAD = 32
CONV_ROWS = 128
CONV_FILL = 128


def _conv_kernel(dv_ref, dg_ref, w_ref, cb_ref, o_ref, dpad_ref):
    s_len = dv_ref.shape[0]
    dpad_ref[0:CONV_PAD, :] = jnp.zeros((CONV_PAD, LANES), F32)

    def fill(i, carry):
        r = pl.multiple_of(i * CONV_FILL, CONV_FILL)
        rows = pl.ds(r, CONV_FILL)
        d = dv_ref[rows, :].astype(F32) * jax.nn.sigmoid(dg_ref[rows, :].astype(F32))
        dpad_ref[pl.ds(CONV_PAD + r, CONV_FILL), :] = d
        return carry

    lax.fori_loop(0, s_len // CONV_FILL, fill, 0)

    lead = CONV_PAD - (CONV_K - 1)
    taps = [w_ref[k:k + 1, :] for k in range(CONV_K)]
    for r in range(0, s_len, CONV_ROWS):
        acc = dpad_ref[r + lead:r + lead + CONV_ROWS, :] * taps[0]
        for k in range(1, CONV_K):
            acc = acc + dpad_ref[r + lead + k:r + lead + k + CONV_ROWS, :] * taps[k]
        o_ref[r:r + CONV_ROWS, :] = acc + cb_ref[...]


def _conv_norm_kernel(c_ref, gate_ref, ng_ref, nb_ref, o_ref):
    acc = c_ref[...]
    mu = jnp.mean(acc, axis=-1, keepdims=True)
    ac = acc - mu
    var = jnp.mean(ac * ac, axis=-1, keepdims=True)
    y = _silu(ac * lax.rsqrt(var + EPS) * ng_ref[...] + nb_ref[...])
    o_ref[...] = (y * _silu(gate_ref[...].astype(F32))).astype(o_ref.dtype)


def _conv_module(z3, col, conv_w, conv_b, cn_g, cn_b, tm=512):
    b, s, cols = z3.shape
    ch = conv_w.shape[1]
    n_slabs = ch // LANES
    blk = col // LANES
    vec = lambda a: a.reshape(1, ch)
    conv = pl.pallas_call(
        _conv_kernel,
        grid=(b, n_slabs),
        in_specs=[pl.BlockSpec((None, s, LANES), lambda i, c: (i, 0, blk + c)),
                  pl.BlockSpec((None, s, LANES), lambda i, c: (i, 0, blk + n_slabs + c)),
                  pl.BlockSpec((CONV_K, LANES), lambda i, c: (0, c)),
                  pl.BlockSpec((1, LANES), lambda i, c: (0, c))],
        out_specs=pl.BlockSpec((None, s, LANES), lambda i, c: (i, 0, c)),
        out_shape=jax.ShapeDtypeStruct((b, s, ch), F32),
        scratch_shapes=[pltpu.VMEM((CONV_PAD + s, LANES), F32)],
        compiler_params=_params(("parallel", "parallel")),
        name="conv_taps",
    )(z3, z3, conv_w, vec(conv_b))
    t = b * s
    fixed2 = lambda i: (0, 0)
    gate_blk = col // ch + 2
    return pl.pallas_call(
        _conv_norm_kernel,
        grid=(t // tm,),
        in_specs=[pl.BlockSpec((tm, ch), lambda i: (i, 0)),
                  pl.BlockSpec((tm, ch), lambda i: (i, gate_blk)),
                  pl.BlockSpec((1, ch), fixed2), pl.BlockSpec((1, ch), fixed2)],
        out_specs=pl.BlockSpec((tm, ch), lambda i: (i, 0)),
        out_shape=jax.ShapeDtypeStruct((t, ch), BF16),
        compiler_params=_params(("parallel",)),
        name="conv_norm",
    )(conv.reshape(t, ch), z3.reshape(t, cols), vec(cn_g), vec(cn_b))


def kernel(x, e_pre_norm, e_w_in, e_pool_w, e_pool_scale, e_w_out, e_post_norm, o_pre_norm, o_w_in, o_sgu_norm_g, o_sgu_norm_b, o_sgu_w, o_sgu_b, o_conv_w, o_conv_b, o_conv_norm_g, o_conv_norm_b, o_w_out, o_post_norm):
    b, s, d = x.shape
    t = b * s
    n_even, n_odd = e_w_in.shape[0], o_w_in.shape[0]
    depth = n_even + n_odd
    pool_width = len(POOL_WINDOWS) * POOL_CH
    n_heads = (e_w_out.shape[1] - pool_width) // HEAD_DIM
    sgu_width = SGU_GROUPS * SGU_CH

    def pre_gain(i):
        return e_pre_norm[i // 2] if i % 2 == 0 else o_pre_norm[i // 2]

    xf = x.reshape(t, d)
    h = _rmsnorm(xf, pre_gain(0))
    for i in range(depth):
        j = i // 2
        g_next = pre_gain(i + 1) if i + 1 < depth else None
        if i % 2 == 0:
            z = _in_proj(h, e_w_in, j)
            z3 = z.reshape(b, s, z.shape[1])
            ya = _pool_mixer(z3, e_pool_w[j], e_pool_scale[j])
            yb = _dilated_attention(z3, 2 * pool_width, n_heads)
            w_out, g_post = e_w_out, e_post_norm[j]
        else:
            z = _in_proj(h, o_w_in, j)
            z3 = z.reshape(b, s, z.shape[1])
            ya = _sgu(z, o_sgu_norm_g[j], o_sgu_norm_b[j], o_sgu_w[j], o_sgu_b[j])
            yb = _conv_module(z3, 3 * sgu_width, o_conv_w[j], o_conv_b[j],
                              o_conv_norm_g[j], o_conv_norm_b[j])
            w_out, g_post = o_w_out, o_post_norm[j]
        outs = _out_proj(ya.reshape(t, -1), yb.reshape(t, -1), w_out, j, xf, g_post, g_next)
        if g_next is None:
            xf = outs[0]
        else:
            xf, h = outs
    return xf.reshape(b, s, d)
```

```python
import functools

import jax
import jax.numpy as jnp
import numpy as np
from jax import lax
from jax.experimental import pallas as pl
from jax.experimental.pallas import tpu as pltpu

F32 = jnp.float32
BF16 = jnp.bfloat16

EPS = 1e-6
NEG = -1e30
HEAD_DIM = 128
ROT_DIM = HEAD_DIM // 4
ROT_HALF = ROT_DIM // 2
ROPE_THETA = 500000.0
DILATIONS = (1, 4, 16)
SPAN = 128
ATT_GROUPS = len(DILATIONS)
POOL_WINDOWS = (2, 4, 8, 16)
POOL_CH = 256
SGU_GROUPS = 4
SGU_CH = 256
CHUNK = 128
CONV_K = 31

LANES = 128
VMEM_LIMIT = 56 * 1024 * 1024


def _params(sem, vmem=VMEM_LIMIT):
    return pltpu.CompilerParams(dimension_semantics=sem, vmem_limit_bytes=vmem)


def _silu(x):
    return x * jax.nn.sigmoid(x)


def _rmsnorm_kernel(x_ref, g_ref, o_ref):
    x = x_ref[...]
    ms = jnp.mean(x * x, axis=-1, keepdims=True)
    o_ref[...] = (x * lax.rsqrt(ms + EPS) * g_ref[...]).astype(o_ref.dtype)


def _rmsnorm(x, g, tm=512):
    t, d = x.shape
    return pl.pallas_call(
        _rmsnorm_kernel,
        grid=(t // tm,),
        in_specs=[pl.BlockSpec((tm, d), lambda i: (i, 0)),
                  pl.BlockSpec((1, d), lambda i: (0, 0))],
        out_specs=pl.BlockSpec((tm, d), lambda i: (i, 0)),
        out_shape=jax.ShapeDtypeStruct((t, d), BF16),
        compiler_params=_params(("parallel",)),
        name="rmsnorm",
    )(x, g.reshape(1, d))


CAST_ROWS = 256


def _cast_weight(w_ref, wb_ref):
    def chunk(i, carry):
        rows = pl.ds(pl.multiple_of(i * CAST_ROWS, CAST_ROWS), CAST_ROWS)
        wb_ref[rows, :] = w_ref[rows, :].astype(wb_ref.dtype)
        return carry

    lax.fori_loop(0, w_ref.shape[0] // CAST_ROWS, chunk, 0)


def _in_proj_kernel(a_ref, w_ref, o_ref, wb_ref):
    first = pl.program_id(1) == 0

    @pl.when(first)
    def _():
        acc = None
        for c in range(0, w_ref.shape[0], CAST_ROWS):
            wc = w_ref[c:c + CAST_ROWS, :].astype(BF16)
            wb_ref[c:c + CAST_ROWS, :] = wc
            part = jnp.dot(a_ref[:, c:c + CAST_ROWS], wc, preferred_element_type=F32)
            acc = part if acc is None else acc + part
        o_ref[...] = acc.astype(o_ref.dtype)

    @pl.when(jnp.logical_not(first))
    def _():
        o_ref[...] = jnp.dot(a_ref[...], wb_ref[...],
                             preferred_element_type=F32).astype(o_ref.dtype)


def _in_proj(a, w_all, layer, tm=2048, tn=1024):
    m, k = a.shape
    n = w_all.shape[2]
    return pl.pallas_call(
        _in_proj_kernel,
        grid=(n // tn, m // tm),
        in_specs=[pl.BlockSpec((tm, k), lambda j, i: (i, 0)),
                  pl.BlockSpec((None, k, tn), lambda j, i: (layer, 0, j))],
        out_specs=pl.BlockSpec((tm, tn), lambda j, i: (i, j)),
        out_shape=jax.ShapeDtypeStruct((m, n), BF16),
        scratch_shapes=[pltpu.VMEM((k, tn), BF16)],
        compiler_params=_params(("parallel", "arbitrary")),
        name="in_proj",
    )(a, w_all)


POOL_PAD = 16
POOL_ROWS = 128
POOL_FILL = 256


def _pool_kernel(z_ref, pw_ref, ps_ref, o_ref, xpad_ref, pwb_ref):
    s_len = z_ref.shape[0]
    width = len(POOL_WINDOWS) * POOL_CH
    n_slabs = width // LANES
    slabs_per_group = POOL_CH // LANES
    for g in range(len(POOL_WINDOWS)):
        pwb_ref[g] = pw_ref[g].astype(BF16)
    for c in range(n_slabs):
        xpad_ref[c, 0:POOL_PAD, :] = jnp.zeros((POOL_PAD, LANES), F32)

    def fill(i, carry):
        r = pl.multiple_of(i * POOL_FILL, POOL_FILL)
        for c in range(n_slabs):
            xpad_ref[c, pl.ds(POOL_PAD + r, POOL_FILL), :] = (
                z_ref[pl.ds(r, POOL_FILL), c * LANES:(c + 1) * LANES].astype(F32))
        return carry

    lax.fori_loop(0, s_len // POOL_FILL, fill, 0)

    for r in range(0, s_len, POOL_ROWS):
        for g, w in enumerate(POOL_WINDOWS):
            cs = slice(g * POOL_CH, (g + 1) * POOL_CH)
            xs, accs = [], []
            for c in range(g * slabs_per_group, (g + 1) * slabs_per_group):
                lo = POOL_PAD + r
                xc = xpad_ref[c, lo:lo + POOL_ROWS, :]
                ac = xc
                for j in range(1, w):
                    ac = ac + xpad_ref[c, lo - j:lo - j + POOL_ROWS, :]
                xs.append(xc)
                accs.append(ac)
            x = jnp.concatenate(xs, axis=-1)
            acc = jnp.concatenate(accs, axis=-1)
            if r + 1 >= w:
                cnt = float(w)
            else:
                row = r + lax.broadcasted_iota(jnp.int32, (POOL_ROWS, POOL_CH), 0)
                cnt = jnp.minimum(row + 1, w).astype(F32)
            pooled = acc / cnt - x
            mixed = jnp.dot(pooled.astype(BF16), pwb_ref[g], preferred_element_type=F32)
            gate = z_ref[r:r + POOL_ROWS, width + g * POOL_CH:width + (g + 1) * POOL_CH].astype(F32)
            y = mixed * ps_ref[:, cs] * _silu(gate)
            o_ref[r:r + POOL_ROWS, cs] = y.astype(o_ref.dtype)


def _pool_mixer(z3, pool_w, pool_scale):
    b, s, _ = z3.shape
    width = len(POOL_WINDOWS) * POOL_CH
    return pl.pallas_call(
        _pool_kernel,
        grid=(b,),
        in_specs=[pl.BlockSpec((None, s, 2 * width), lambda i: (i, 0, 0)),
                  pl.BlockSpec(pool_w.shape, lambda i: (0, 0, 0)),
                  pl.BlockSpec((1, width), lambda i: (0, 0))],
        out_specs=pl.BlockSpec((None, s, width), lambda i: (i, 0, 0)),
        out_shape=jax.ShapeDtypeStruct((b, s, width), BF16),
        scratch_shapes=[pltpu.VMEM((width // LANES, POOL_PAD + s, LANES), F32),
                        pltpu.VMEM(pool_w.shape, BF16)],
        compiler_params=_params(("parallel",)),
        name="pool_mixer",
    )(z3, pool_w, pool_scale.reshape(1, width))


ATT_ROWS = 1024
ATT_PREP_ROWS = 2048
ATT_MAX_BATCH = 16
ATT_INTERLEAVE = 16
assert all(d <= ATT_MAX_BATCH for d in DILATIONS)


def _largest_divisor(n, cap):
    return max(u for u in range(1, cap + 1) if n % u == 0)


def _for_blocks(n, fn):
    per = _largest_divisor(n, ATT_INTERLEAVE)

    def body(it, carry):
        for u in range(per):
            fn(it * per + u)
        return carry

    if n == per:
        body(0, 0)
    else:
        lax.fori_loop(0, n // per, body, 0)


def _for_chunks(n, body):
    if n == 1:
        body(0, 0)
    else:
        lax.fori_loop(0, n, body, 0)


def _aligned(start, align):
    return start if isinstance(start, int) else pl.multiple_of(start, align)


def _attn_kernel(q0, k0, v0, q1, k1, v1, q2, k2, v2, gate_ref, cos_ref, sin_ref,
                 o_ref, qkd_ref, qkv_ref, s_ref, ms_ref, acc_ref, m_ref, l_ref):
    s_len = o_ref.shape[0]
    qkv_in = ((q0, k0, v0), (q1, k1, v1), (q2, k2, v2))

    src = lax.broadcasted_iota(jnp.int32, (HEAD_DIM, HEAD_DIM), 0)
    dst = lax.broadcasted_iota(jnp.int32, (HEAD_DIM, HEAD_DIM), 1)
    perm = (jnp.where((dst < ROT_HALF) & (src == dst + ROT_HALF), -1.0, 0.0)
            + jnp.where((dst >= ROT_HALF) & (dst < ROT_DIM) & (src == dst - ROT_HALF), 1.0, 0.0)
            ).astype(BF16)

    strided = [g for g, d in enumerate(DILATIONS) if d > 1]
    dense = [g for g, d in enumerate(DILATIONS) if d == 1]

    prep_rows, merge_rows = min(ATT_PREP_ROWS, s_len), min(ATT_ROWS, s_len)

    def prep(i, c):
        chunk = pl.ds(_aligned(i * prep_rows, prep_rows), prep_rows)
        cos, sin = cos_ref[chunk, :], sin_ref[chunk, :]
        for g, d in enumerate(DILATIONS):
            for j in range(2):
                t = qkv_in[g][j][chunk, :]
                partner = jnp.dot(t, perm, preferred_element_type=F32)
                roped = t.astype(F32) * cos + partner * sin
                if d == 1:
                    qkd_ref[2 * dense.index(g) + j, chunk, :] = roped.astype(BF16)
                else:
                    qkv_ref[3 * strided.index(g) + j, chunk, :] = roped
            if d > 1:
                qkv_ref[3 * strided.index(g) + 2, chunk, :] = qkv_in[g][2][chunk, :].astype(F32)
        return c

    _for_chunks(s_len // prep_rows, prep)

    qi = lax.broadcasted_iota(jnp.int32, (SPAN, 2 * SPAN), 0)
    kj = lax.broadcasted_iota(jnp.int32, (SPAN, 2 * SPAN), 1)
    band = (kj >= qi) & (kj <= qi + SPAN)
    causal = (lax.broadcasted_iota(jnp.int32, (SPAN, SPAN), 1)
              <= lax.broadcasted_iota(jnp.int32, (SPAN, SPAN), 0))

    def rows(d, start, cnt):
        return pl.ds(_aligned(start, SPAN), cnt) if d == 1 else pl.ds(start, cnt, stride=d)

    def operand(g, d, j, start, cnt):
        if d == 1:
            src = qkv_in[g][2] if j == 2 else qkd_ref.at[2 * dense.index(g) + j]
            return src[rows(d, start, cnt), :]
        return qkv_ref[3 * strided.index(g) + j, rows(d, start, cnt), :].astype(BF16)

    def run_batch(n, n_keys, mask, spec):
        assert n <= ATT_MAX_BATCH

        def blk(j):
            return pl.ds(_aligned(j * SPAN, SPAN), SPAN)

        def scores(j):
            g, d, q_start, k_start = spec(j)
            s = lax.dot_general(operand(g, d, 0, q_start, SPAN), operand(g, d, 1, k_start, n_keys),
                                (((1,), (1,)), ((), ())), preferred_element_type=F32)
            s = jnp.where(mask, s * (HEAD_DIM ** -0.5), NEG)
            s_ref[blk(j), 0:n_keys] = s
            m = jnp.max(s, axis=-1, keepdims=True)
            ms_ref[blk(j), :] = jnp.broadcast_to(m, (SPAN, HEAD_DIM))

        def outputs(j):
            g, d, q_start, k_start = spec(j)
            m = ms_ref[blk(j), :]
            den, ps = None, []
            for c in range(n_keys // HEAD_DIM):
                p = jnp.exp(s_ref[blk(j), c * HEAD_DIM:(c + 1) * HEAD_DIM] - m)
                ps.append(p.astype(BF16))
                den = p if den is None else den + p
            den = jnp.sum(den, axis=-1, keepdims=True)
            m_ref[g, rows(d, q_start, SPAN), :] = m
            l_ref[g, rows(d, q_start, SPAN), :] = jnp.broadcast_to(den, (SPAN, HEAD_DIM))
            pb = ps[0] if len(ps) == 1 else jnp.concatenate(ps, axis=1)
            acc_ref[g, rows(d, q_start, SPAN), :] = jnp.dot(
                pb, operand(g, d, 2, k_start, n_keys), preferred_element_type=F32)

        for stage in (scores, outputs):
            _for_blocks(n, stage)

    few = [(g, d, r, r) for g, d in enumerate(DILATIONS) if d <= ATT_INTERLEAVE for r in range(d)]
    if few:
        for lo in range(0, len(few), ATT_INTERLEAVE):
            part = few[lo:lo + ATT_INTERLEAVE]
            run_batch(len(part), SPAN, causal, lambda j, part=part: part[j])
    for g, d in enumerate(DILATIONS):
        if d > ATT_INTERLEAVE:
            run_batch(d, SPAN, causal, lambda j, g=g, d=d: (g, d, j, j))

    for g, d in enumerate(DILATIONS):
        n_blocks = s_len // (d * SPAN)
        stride_blk = d * SPAN

        def spec(j, g=g, d=d, stride_blk=stride_blk):
            if d == 1:
                return g, d, (1 + j) * SPAN, j * SPAN
            if isinstance(j, int):
                r, i = j % d, 1 + j // d
            else:
                r, i = lax.rem(j, d), 1 + lax.div(j, d)
            q_start = r + i * stride_blk
            return g, d, q_start, q_start - stride_blk

        if n_blocks > 1:
            run_batch(d * (n_blocks - 1), 2 * SPAN, band, spec)

    def merge(i, c):
        rows = pl.ds(_aligned(i * merge_rows, merge_rows), merge_rows)
        ms = [m_ref[g, rows, :] for g in range(ATT_GROUPS)]
        m = functools.reduce(jnp.maximum, ms)
        num = den = None
        for g in range(ATT_GROUPS):
            e = jnp.exp(ms[g] - m)
            ng, dg = e * acc_ref[g, rows, :], e * l_ref[g, rows, :]
            num, den = (ng, dg) if num is None else (num + ng, den + dg)
        gate = gate_ref[rows, :].astype(F32)
        o_ref[rows, :] = (num / den * _silu(gate)).astype(o_ref.dtype)
        return c

    _for_chunks(s_len // merge_rows, merge)


def _rope_tables(s_len):
    pos = jnp.arange(s_len, dtype=F32)
    inv_freq = jnp.power(ROPE_THETA, -jnp.arange(0, ROT_DIM, 2, dtype=F32) / ROT_DIM)
    ang = pos[:, None] * inv_freq[None, :]
    cos, sin = jnp.cos(ang), jnp.sin(ang)
    rest = jnp.zeros((s_len, HEAD_DIM - ROT_DIM), F32)
    return (jnp.concatenate([cos, cos, rest + 1.0], axis=-1),
            jnp.concatenate([sin, sin, rest], axis=-1))


def _dilated_attention(z3, q_col, n_heads):
    b, s, _ = z3.shape
    per = ATT_GROUPS * n_heads
    qb = q_col // HEAD_DIM
    cos_t, sin_t = _rope_tables(s)

    def head_spec(blk0):
        return pl.BlockSpec((None, s, HEAD_DIM), lambda i, h, blk0=blk0: (i, 0, blk0 + h))

    in_specs = []
    for g in range(ATT_GROUPS):
        for j in range(3):
            in_specs.append(head_spec(qb + j * per + g * n_heads))
    in_specs.append(head_spec(qb + 3 * per))
    tab_spec = pl.BlockSpec((s, HEAD_DIM), lambda i, h: (0, 0))
    in_specs += [tab_spec, tab_spec]
    return pl.pallas_call(
        _attn_kernel,
        grid=(b, n_heads),
        in_specs=in_specs,
        out_specs=pl.BlockSpec((None, s, HEAD_DIM), lambda i, h: (i, 0, h)),
        out_shape=jax.ShapeDtypeStruct((b, s, n_heads * HEAD_DIM), BF16),
        scratch_shapes=[pltpu.VMEM((2 * sum(d == 1 for d in DILATIONS), s, HEAD_DIM), BF16),
                        pltpu.VMEM((3 * sum(d > 1 for d in DILATIONS), s, HEAD_DIM), F32),
                        pltpu.VMEM((ATT_MAX_BATCH * SPAN, 2 * SPAN), F32),
                        pltpu.VMEM((ATT_MAX_BATCH * SPAN, HEAD_DIM), F32),
                        pltpu.VMEM((ATT_GROUPS, s, HEAD_DIM), F32),
                        pltpu.VMEM((ATT_GROUPS, s, HEAD_DIM), F32),
                        pltpu.VMEM((ATT_GROUPS, s, HEAD_DIM), F32)],
        compiler_params=_params(("parallel", "parallel")),
        name="dilated_attention",
    )(*([z3] * 10), cos_t, sin_t)


OUT_SUB_ROWS = 128


def _out_proj_kernel(*refs, with_next, conv_in):
    refs = list(refs)
    ya_ref = refs.pop(0)
    if conv_in:
        conv_ref, gate_ref, ng_ref, nb_ref = refs[:4]
        del refs[:4]
    else:
        yb_ref = refs.pop(0)
    w_ref, x_ref, gpost_ref = refs[:3]
    del refs[:3]
    if with_next:
        gnext_ref, xo_ref, ho_ref, wb_ref = refs
    else:
        xo_ref, wb_ref = refs

    @pl.when(pl.program_id(0) == 0)
    def _():
        _cast_weight(w_ref, wb_ref)

    half = ya_ref.shape[1]
    for r in range(0, ya_ref.shape[0], OUT_SUB_ROWS):
        rows = slice(r, r + OUT_SUB_ROWS)
        if conv_in:
            c = conv_ref[rows, :]
            mu = jnp.mean(c, axis=-1, keepdims=True)
            cc = c - mu
            var = jnp.mean(cc * cc, axis=-1, keepdims=True)
            yb = _silu(cc * lax.rsqrt(var + EPS) * ng_ref[...] + nb_ref[...])
            yb = (yb * _silu(gate_ref[rows, :].astype(F32))).astype(BF16)
        else:
            yb = yb_ref[rows, :]
        y = jnp.dot(ya_ref[rows, :], wb_ref[0:half, :], preferred_element_type=F32)
        y = y + jnp.dot(yb, wb_ref[half:2 * half, :], preferred_element_type=F32)
        yn = y * lax.rsqrt(jnp.mean(y * y, axis=-1, keepdims=True) + EPS) * gpost_ref[...]
        xn = x_ref[rows, :] + yn
        xo_ref[rows, :] = xn
        if with_next:
            hn = xn * lax.rsqrt(jnp.mean(xn * xn, axis=-1, keepdims=True) + EPS) * gnext_ref[...]
            ho_ref[rows, :] = hn.astype(ho_ref.dtype)


def _out_proj(ya, yb, w_all, layer, x, g_post, g_next=None, tm=512):
    t, half = ya.shape
    _, k, d = w_all.shape
    with_next = g_next is not None
    conv_in = isinstance(yb, tuple)
    row = lambda i: (i, 0)
    fixed = lambda i: (0, 0)
    in_specs = [pl.BlockSpec((tm, half), row)]
    args = [ya]
    if conv_in:
        conv, z, gate_blk, cn_g, cn_b = yb
        in_specs += [pl.BlockSpec((tm, half), row),
                     pl.BlockSpec((tm, half), lambda i: (i, gate_blk)),
                     pl.BlockSpec((1, half), fixed), pl.BlockSpec((1, half), fixed)]
        args += [conv, z, cn_g.reshape(1, half), cn_b.reshape(1, half)]
    else:
        in_specs.append(pl.BlockSpec((tm, half), row))
        args.append(yb)
    in_specs += [pl.BlockSpec((None, k, d), lambda i: (layer, 0, 0), pipeline_mode=pl.Buffered(1)),
                 pl.BlockSpec((tm, d), row),
                 pl.BlockSpec((1, d), fixed)]
    args += [w_all, x, g_post.reshape(1, d)]
    out_specs = [pl.BlockSpec((tm, d), row)]
    out_shape = [jax.ShapeDtypeStruct((t, d), F32)]
    if with_next:
        in_specs.append(pl.BlockSpec((1, d), fixed))
        args.append(g_next.reshape(1, d))
        out_specs.append(pl.BlockSpec((tm, d), row))
        out_shape.append(jax.ShapeDtypeStruct((t, d), BF16))
    return pl.pallas_call(
        functools.partial(_out_proj_kernel, with_next=with_next, conv_in=conv_in),
        grid=(t // tm,),
        in_specs=in_specs,
        out_specs=out_specs,
        out_shape=out_shape,
        scratch_shapes=[pltpu.VMEM((k, d), BF16)],
        compiler_params=_params(("arbitrary",)),
        name="out_proj",
    )(*args)


def _sgu_kernel(u_ref, v_ref, cg_ref, g_ref, b_ref, ws_ref, bias_ref, o_ref):
    tm = u_ref.shape[0]
    ii = lax.broadcasted_iota(jnp.int32, (CHUNK, CHUNK), 0)
    jj = lax.broadcasted_iota(jnp.int32, (CHUNK, CHUNK), 1)
    tril = jj <= ii
    for c in range(tm // CHUNK):
        rows = slice(c * CHUNK, (c + 1) * CHUNK)
        v = v_ref[rows, :].astype(F32)
        mu = jnp.mean(v, axis=-1, keepdims=True)
        vc = v - mu
        var = jnp.mean(vc * vc, axis=-1, keepdims=True)
        vn = (vc * lax.rsqrt(var + EPS) * g_ref[...] + b_ref[...]).astype(BF16)
        for h in range(SGU_GROUPS):
            cs = slice(h * SGU_CH, (h + 1) * SGU_CH)
            wm = jnp.where(tril, ws_ref[h], 0.0).astype(BF16)
            sg = jnp.dot(wm, vn[:, cs], preferred_element_type=F32) + bias_ref[:, cs]
            y = u_ref[rows, cs].astype(F32) * sg * _silu(cg_ref[rows, cs].astype(F32))
            o_ref[rows, cs] = y.astype(o_ref.dtype)


def _sgu(z, sgu_g, sgu_b, sgu_w, sgu_bias, tm=512):
    t = z.shape[0]
    width = SGU_GROUPS * SGU_CH
    bias_full = jnp.repeat(sgu_bias.T, SGU_CH, axis=1)
    fixed2 = lambda i: (0, 0)
    return pl.pallas_call(
        _sgu_kernel,
        grid=(t // tm,),
        in_specs=[pl.BlockSpec((tm, width), lambda i: (i, 0)),
                  pl.BlockSpec((tm, width), lambda i: (i, 1)),
                  pl.BlockSpec((tm, width), lambda i: (i, 2)),
                  pl.BlockSpec((1, width), fixed2), pl.BlockSpec((1, width), fixed2),
                  pl.BlockSpec(sgu_w.shape, lambda i: (0, 0, 0)),
                  pl.BlockSpec((CHUNK, width), fixed2)],
        out_specs=pl.BlockSpec((tm, width), lambda i: (i, 0)),
        out_shape=jax.ShapeDtypeStruct((t, width), BF16),
        compiler_params=_params(("parallel",)),
        name="sgu",
    )(z, z, z, sgu_g.reshape(1, width), sgu_b.reshape(1, width), sgu_w, bias_full)


CONV_PAD = 32
CONV_ROWS = 128
CONV_FILL = 128


def _conv_kernel(dv_ref, dg_ref, w_ref, cb_ref, o_ref, dpad_ref):
    s_len = dv_ref.shape[0]
    dpad_ref[0:CONV_PAD, :] = jnp.zeros((CONV_PAD, LANES), F32)

    def fill(i, carry):
        r = pl.multiple_of(i * CONV_FILL, CONV_FILL)
        rows = pl.ds(r, CONV_FILL)
        d = dv_ref[rows, :].astype(F32) * jax.nn.sigmoid(dg_ref[rows, :].astype(F32))
        dpad_ref[pl.ds(CONV_PAD + r, CONV_FILL), :] = d
        return carry

    lax.fori_loop(0, s_len // CONV_FILL, fill, 0)

    lead = CONV_PAD - (CONV_K - 1)
    taps = [w_ref[k:k + 1, :] for k in range(CONV_K)]
    for r in range(0, s_len, CONV_ROWS):
        acc = dpad_ref[r + lead:r + lead + CONV_ROWS, :] * taps[0]
        for k in range(1, CONV_K):
            acc = acc + dpad_ref[r + lead + k:r + lead + k + CONV_ROWS, :] * taps[k]
        o_ref[r:r + CONV_ROWS, :] = acc + cb_ref[...]


def _conv_taps(z3, col, conv_w, conv_b):
    b, s, cols = z3.shape
    ch = conv_w.shape[1]
    n_slabs = ch // LANES
    blk = col // LANES
    vec = lambda a: a.reshape(1, ch)
    return pl.pallas_call(
        _conv_kernel,
        grid=(b, n_slabs),
        in_specs=[pl.BlockSpec((None, s, LANES), lambda i, c: (i, 0, blk + c)),
                  pl.BlockSpec((None, s, LANES), lambda i, c: (i, 0, blk + n_slabs + c)),
                  pl.BlockSpec((CONV_K, LANES), lambda i, c: (0, c)),
                  pl.BlockSpec((1, LANES), lambda i, c: (0, c))],
        out_specs=pl.BlockSpec((None, s, LANES), lambda i, c: (i, 0, c)),
        out_shape=jax.ShapeDtypeStruct((b, s, ch), F32),
        scratch_shapes=[pltpu.VMEM((CONV_PAD + s, LANES), F32)],
        compiler_params=_params(("parallel", "parallel")),
        name="conv_taps",
    )(z3, z3, conv_w, vec(conv_b))


def kernel(x, e_pre_norm, e_w_in, e_pool_w, e_pool_scale, e_w_out, e_post_norm, o_pre_norm, o_w_in, o_sgu_norm_g, o_sgu_norm_b, o_sgu_w, o_sgu_b, o_conv_w, o_conv_b, o_conv_norm_g, o_conv_norm_b, o_w_out, o_post_norm):
    b, s, d = x.shape
    t = b * s
    n_even, n_odd = e_w_in.shape[0], o_w_in.shape[0]
    depth = n_even + n_odd
    pool_width = len(POOL_WINDOWS) * POOL_CH
    n_heads = (e_w_out.shape[1] - pool_width) // HEAD_DIM
    sgu_width = SGU_GROUPS * SGU_CH

    def pre_gain(i):
        return e_pre_norm[i // 2] if i % 2 == 0 else o_pre_norm[i // 2]

    xf = x.reshape(t, d)
    h = _rmsnorm(xf, pre_gain(0))
    for i in range(depth):
        j = i // 2
        g_next = pre_gain(i + 1) if i + 1 < depth else None
        if i % 2 == 0:
            z = _in_proj(h, e_w_in, j)
            z3 = z.reshape(b, s, z.shape[1])
            ya = _pool_mixer(z3, e_pool_w[j], e_pool_scale[j])
            yb = _dilated_attention(z3, 2 * pool_width, n_heads)
            w_out, g_post = e_w_out, e_post_norm[j]
        else:
            z = _in_proj(h, o_w_in, j)
            z3 = z.reshape(b, s, z.shape[1])
            ya = _sgu(z, o_sgu_norm_g[j], o_sgu_norm_b[j], o_sgu_w[j], o_sgu_b[j])
            conv_width = o_conv_w.shape[2]
            conv = _conv_taps(z3, 3 * sgu_width, o_conv_w[j], o_conv_b[j])
            gate_blk = (3 * sgu_width + 2 * conv_width) // conv_width
            yb = (conv.reshape(t, conv_width), z, gate_blk, o_conv_norm_g[j], o_conv_norm_b[j])
            w_out, g_post = o_w_out, o_post_norm[j]
        if not isinstance(yb, tuple):
            yb = yb.reshape(t, -1)
        outs = _out_proj(ya.reshape(t, -1), yb, w_out, j, xf, g_post, g_next)
        if g_next is None:
            xf = outs[0]
        else:
            xf, h = outs
    return xf.reshape(b, s, d)
```

```python
import functools

import jax
import jax.numpy as jnp
import numpy as np
from jax import lax
from jax.experimental import pallas as pl
from jax.experimental.pallas import tpu as pltpu

F32 = jnp.float32
BF16 = jnp.bfloat16

EPS = 1e-6
NEG = -1e30
HEAD_DIM = 128
ROT_DIM = HEAD_DIM // 4
ROT_HALF = ROT_DIM // 2
ROPE_THETA = 500000.0
DILATIONS = (1, 4, 16)
SPAN = 128
ATT_GROUPS = len(DILATIONS)
POOL_WINDOWS = (2, 4, 8, 16)
POOL_CH = 256
SGU_GROUPS = 4
SGU_CH = 256
CHUNK = 128
CONV_K = 31

LANES = 128
VMEM_LIMIT = 56 * 1024 * 1024


def _params(sem, vmem=VMEM_LIMIT):
    return pltpu.CompilerParams(dimension_semantics=sem, vmem_limit_bytes=vmem)


def _silu(x):
    return x * jax.nn.sigmoid(x)


def _rmsnorm_kernel(x_ref, g_ref, o_ref):
    x = x_ref[...]
    ms = jnp.mean(x * x, axis=-1, keepdims=True)
    o_ref[...] = (x * lax.rsqrt(ms + EPS) * g_ref[...]).astype(o_ref.dtype)


def _rmsnorm(x, g, tm=512):
    t, d = x.shape
    return pl.pallas_call(
        _rmsnorm_kernel,
        grid=(t // tm,),
        in_specs=[pl.BlockSpec((tm, d), lambda i: (i, 0)),
                  pl.BlockSpec((1, d), lambda i: (0, 0))],
        out_specs=pl.BlockSpec((tm, d), lambda i: (i, 0)),
        out_shape=jax.ShapeDtypeStruct((t, d), BF16),
        compiler_params=_params(("parallel",)),
        name="rmsnorm",
    )(x, g.reshape(1, d))


CAST_ROWS = 256


def _cast_weight(w_ref, wb_ref):
    def chunk(i, carry):
        rows = pl.ds(pl.multiple_of(i * CAST_ROWS, CAST_ROWS), CAST_ROWS)
        wb_ref[rows, :] = w_ref[rows, :].astype(wb_ref.dtype)
        return carry

    lax.fori_loop(0, w_ref.shape[0] // CAST_ROWS, chunk, 0)


def _in_proj_kernel(a_ref, w_ref, o_ref, wb_ref):
    first = pl.program_id(1) == 0

    @pl.when(first)
    def _():
        acc = None
        for c in range(0, w_ref.shape[0], CAST_ROWS):
            wc = w_ref[c:c + CAST_ROWS, :].astype(BF16)
            wb_ref[c:c + CAST_ROWS, :] = wc
            part = jnp.dot(a_ref[:, c:c + CAST_ROWS], wc, preferred_element_type=F32)
            acc = part if acc is None else acc + part
        o_ref[...] = acc.astype(o_ref.dtype)

    @pl.when(jnp.logical_not(first))
    def _():
        o_ref[...] = jnp.dot(a_ref[...], wb_ref[...],
                             preferred_element_type=F32).astype(o_ref.dtype)


def _in_proj(a, w_all, layer, tm=2048, tn=1024):
    m, k = a.shape
    n = w_all.shape[2]
    return pl.pallas_call(
        _in_proj_kernel,
        grid=(n // tn, m // tm),
        in_specs=[pl.BlockSpec((tm, k), lambda j, i: (i, 0)),
                  pl.BlockSpec((None, k, tn), lambda j, i: (layer, 0, j))],
        out_specs=pl.BlockSpec((tm, tn), lambda j, i: (i, j)),
        out_shape=jax.ShapeDtypeStruct((m, n), BF16),
        scratch_shapes=[pltpu.VMEM((k, tn), BF16)],
        compiler_params=_params(("parallel", "arbitrary")),
        name="in_proj",
    )(a, w_all)


POOL_PAD = 16
POOL_ROWS = 128
POOL_FILL = 256


def _pool_kernel(z_ref, pw_ref, ps_ref, o_ref, xpad_ref, pwb_ref):
    s_len = z_ref.shape[0]
    width = len(POOL_WINDOWS) * POOL_CH
    n_slabs = width // LANES
    slabs_per_group = POOL_CH // LANES
    for g in range(len(POOL_WINDOWS)):
        pwb_ref[g] = pw_ref[g].astype(BF16)
    for c in range(n_slabs):
        xpad_ref[c, 0:POOL_PAD, :] = jnp.zeros((POOL_PAD, LANES), F32)

    def fill(i, carry):
        r = pl.multiple_of(i * POOL_FILL, POOL_FILL)
        for c in range(n_slabs):
            xpad_ref[c, pl.ds(POOL_PAD + r, POOL_FILL), :] = (
                z_ref[pl.ds(r, POOL_FILL), c * LANES:(c + 1) * LANES].astype(F32))
        return carry

    lax.fori_loop(0, s_len // POOL_FILL, fill, 0)

    for r in range(0, s_len, POOL_ROWS):
        for g, w in enumerate(POOL_WINDOWS):
            cs = slice(g * POOL_CH, (g + 1) * POOL_CH)
            xs, accs = [], []
            for c in range(g * slabs_per_group, (g + 1) * slabs_per_group):
                lo = POOL_PAD + r
                xc = xpad_ref[c, lo:lo + POOL_ROWS, :]
                ac = xc
                for j in range(1, w):
                    ac = ac + xpad_ref[c, lo - j:lo - j + POOL_ROWS, :]
                xs.append(xc)
                accs.append(ac)
            x = jnp.concatenate(xs, axis=-1)
            acc = jnp.concatenate(accs, axis=-1)
            if r + 1 >= w:
                cnt = float(w)
            else:
                row = r + lax.broadcasted_iota(jnp.int32, (POOL_ROWS, POOL_CH), 0)
                cnt = jnp.minimum(row + 1, w).astype(F32)
            pooled = acc / cnt - x
            mixed = jnp.dot(pooled.astype(BF16), pwb_ref[g], preferred_element_type=F32)
            gate = z_ref[r:r + POOL_ROWS, width + g * POOL_CH:width + (g + 1) * POOL_CH].astype(F32)
            y = mixed * ps_ref[:, cs] * _silu(gate)
            o_ref[r:r + POOL_ROWS, cs] = y.astype(o_ref.dtype)


def _pool_mixer(z3, pool_w, pool_scale):
    b, s, _ = z3.shape
    width = len(POOL_WINDOWS) * POOL_CH
    return pl.pallas_call(
        _pool_kernel,
        grid=(b,),
        in_specs=[pl.BlockSpec((None, s, 2 * width), lambda i: (i, 0, 0)),
                  pl.BlockSpec(pool_w.shape, lambda i: (0, 0, 0)),
                  pl.BlockSpec((1, width), lambda i: (0, 0))],
        out_specs=pl.BlockSpec((None, s, width), lambda i: (i, 0, 0)),
        out_shape=jax.ShapeDtypeStruct((b, s, width), BF16),
        scratch_shapes=[pltpu.VMEM((width // LANES, POOL_PAD + s, LANES), F32),
                        pltpu.VMEM(pool_w.shape, BF16)],
        compiler_params=_params(("parallel",)),
        name="pool_mixer",
    )(z3, pool_w, pool_scale.reshape(1, width))


ATT_SCALE_LOG2 = HEAD_DIM ** -0.5 * float(np.log2(np.e))
ATT_ROWS = 1024
ATT_PREP_ROWS = 2048
ATT_MAX_BATCH = 16
ATT_INTERLEAVE = 16
assert all(d <= ATT_MAX_BATCH for d in DILATIONS)


def _largest_divisor(n, cap):
    return max(u for u in range(1, cap + 1) if n % u == 0)


def _for_blocks(n, fn):
    per = _largest_divisor(n, ATT_INTERLEAVE)

    def body(it, carry):
        for u in range(per):
            fn(it * per + u)
        return carry

    if n == per:
        body(0, 0)
    else:
        lax.fori_loop(0, n // per, body, 0)


def _for_chunks(n, body):
    if n == 1:
        body(0, 0)
    else:
        lax.fori_loop(0, n, body, 0)


def _aligned(start, align):
    return start if isinstance(start, int) else pl.multiple_of(start, align)


def _attn_kernel(q0, k0, v0, q1, k1, v1, q2, k2, v2, gate_ref, cos_ref, sin_ref,
                 o_ref, qkd_ref, qkv_ref, bias_ref, s_ref, ms_ref, acc_ref, m_ref, l_ref):
    s_len = o_ref.shape[0]
    qkv_in = ((q0, k0, v0), (q1, k1, v1), (q2, k2, v2))

    src = lax.broadcasted_iota(jnp.int32, (HEAD_DIM, HEAD_DIM), 0)
    dst = lax.broadcasted_iota(jnp.int32, (HEAD_DIM, HEAD_DIM), 1)
    perm = (jnp.where((dst < ROT_HALF) & (src == dst + ROT_HALF), -1.0, 0.0)
            + jnp.where((dst >= ROT_HALF) & (dst < ROT_DIM) & (src == dst - ROT_HALF), 1.0, 0.0)
            ).astype(BF16)

    strided = [g for g, d in enumerate(DILATIONS) if d > 1]
    dense = [g for g, d in enumerate(DILATIONS) if d == 1]

    prep_rows, merge_rows = min(ATT_PREP_ROWS, s_len), min(ATT_ROWS, s_len)

    def prep(i, c):
        chunk = pl.ds(_aligned(i * prep_rows, prep_rows), prep_rows)
        cos, sin = cos_ref[chunk, :], sin_ref[chunk, :]
        for g, d in enumerate(DILATIONS):
            for j in range(2):
                t = qkv_in[g][j][chunk, :]
                partner = jnp.dot(t, perm, preferred_element_type=F32)
                roped = t.astype(F32) * cos + partner * sin
                if d == 1:
                    qkd_ref[2 * dense.index(g) + j, chunk, :] = roped.astype(BF16)
                else:
                    qkv_ref[3 * strided.index(g) + j, chunk, :] = roped
            if d > 1:
                qkv_ref[3 * strided.index(g) + 2, chunk, :] = qkv_in[g][2][chunk, :].astype(F32)
        return c

    _for_chunks(s_len // prep_rows, prep)

    qi = lax.broadcasted_iota(jnp.int32, (SPAN, 2 * SPAN), 0)
    kj = lax.broadcasted_iota(jnp.int32, (SPAN, 2 * SPAN), 1)
    band, causal = 0, 1
    bias_ref[band] = jnp.where((kj >= qi) & (kj <= qi + SPAN), 0.0, NEG)
    bias_ref[causal] = jnp.where(kj <= qi, 0.0, NEG)

    def rows(d, start, cnt):
        return pl.ds(_aligned(start, SPAN), cnt) if d == 1 else pl.ds(start, cnt, stride=d)

    def operand(g, d, j, start, cnt):
        if d == 1:
            src = qkv_in[g][2] if j == 2 else qkd_ref.at[2 * dense.index(g) + j]
            return src[rows(d, start, cnt), :]
        return qkv_ref[3 * strided.index(g) + j, rows(d, start, cnt), :].astype(BF16)

    def run_batch(n, n_keys, mask, spec):
        assert n <= ATT_MAX_BATCH

        def blk(j):
            return pl.ds(_aligned(j * SPAN, SPAN), SPAN)

        def scores(j):
            g, d, q_start, k_start = spec(j)
            s = lax.dot_general(operand(g, d, 0, q_start, SPAN), operand(g, d, 1, k_start, n_keys),
                                (((1,), (1,)), ((), ())), preferred_element_type=F32)
            s = s * ATT_SCALE_LOG2 + bias_ref[mask, :, 0:n_keys]
            s_ref[blk(j), 0:n_keys] = s
            m = jnp.max(s, axis=-1, keepdims=True)
            ms_ref[blk(j), :] = jnp.broadcast_to(m, (SPAN, HEAD_DIM))

        def outputs(j):
            g, d, q_start, k_start = spec(j)
            m = ms_ref[blk(j), :]
            den, ps = None, []
            for c in range(n_keys // HEAD_DIM):
                p = jnp.exp2(s_ref[blk(j), c * HEAD_DIM:(c + 1) * HEAD_DIM] - m)
                ps.append(p.astype(BF16))
                den = p if den is None else den + p
            den = jnp.sum(den, axis=-1, keepdims=True)
            m_ref[g, rows(d, q_start, SPAN), :] = m
            l_ref[g, rows(d, q_start, SPAN), :] = jnp.broadcast_to(den, (SPAN, HEAD_DIM))
            pb = ps[0] if len(ps) == 1 else jnp.concatenate(ps, axis=1)
            acc_ref[g, rows(d, q_start, SPAN), :] = jnp.dot(
                pb, operand(g, d, 2, k_start, n_keys), preferred_element_type=F32)

        for stage in (scores, outputs):
            _for_blocks(n, stage)

    few = [(g, d, r, r) for g, d in enumerate(DILATIONS) if d <= ATT_INTERLEAVE for r in range(d)]
    if few:
        for lo in range(0, len(few), ATT_INTERLEAVE):
            part = few[lo:lo + ATT_INTERLEAVE]
            run_batch(len(part), SPAN, causal, lambda j, part=part: part[j])
    for g, d in enumerate(DILATIONS):
        if d > ATT_INTERLEAVE:
            run_batch(d, SPAN, causal, lambda j, g=g, d=d: (g, d, j, j))

    for g, d in enumerate(DILATIONS):
        n_blocks = s_len // (d * SPAN)
        stride_blk = d * SPAN

        def spec(j, g=g, d=d, stride_blk=stride_blk):
            if d == 1:
                return g, d, (1 + j) * SPAN, j * SPAN
            if isinstance(j, int):
                r, i = j % d, 1 + j // d
            else:
                r, i = lax.rem(j, d), 1 + lax.div(j, d)
            q_start = r + i * stride_blk
            return g, d, q_start, q_start - stride_blk

        if n_blocks > 1:
            run_batch(d * (n_blocks - 1), 2 * SPAN, band, spec)

    def merge(i, c):
        rows = pl.ds(_aligned(i * merge_rows, merge_rows), merge_rows)
        ms = [m_ref[g, rows, :] for g in range(ATT_GROUPS)]
        m = functools.reduce(jnp.maximum, ms)
        num = den = None
        for g in range(ATT_GROUPS):
            e = jnp.exp2(ms[g] - m)
            ng, dg = e * acc_ref[g, rows, :], e * l_ref[g, rows, :]
            num, den = (ng, dg) if num is None else (num + ng, den + dg)
        gate = gate_ref[rows, :].astype(F32)
        o_ref[rows, :] = (num / den * _silu(gate)).astype(o_ref.dtype)
        return c

    _for_chunks(s_len // merge_rows, merge)


def _rope_tables(s_len):
    pos = jnp.arange(s_len, dtype=F32)
    inv_freq = jnp.power(ROPE_THETA, -jnp.arange(0, ROT_DIM, 2, dtype=F32) / ROT_DIM)
    ang = pos[:, None] * inv_freq[None, :]
    cos, sin = jnp.cos(ang), jnp.sin(ang)
    rest = jnp.zeros((s_len, HEAD_DIM - ROT_DIM), F32)
    return (jnp.concatenate([cos, cos, rest + 1.0], axis=-1),
            jnp.concatenate([sin, sin, rest], axis=-1))


def _dilated_attention(z3, q_col, n_heads):
    b, s, _ = z3.shape
    per = ATT_GROUPS * n_heads
    qb = q_col // HEAD_DIM
    cos_t, sin_t = _rope_tables(s)

    def head_spec(blk0):
        return pl.BlockSpec((None, s, HEAD_DIM), lambda i, h, blk0=blk0: (i, 0, blk0 + h))

    in_specs = []
    for g in range(ATT_GROUPS):
        for j in range(3):
            in_specs.append(head_spec(qb + j * per + g * n_heads))
    in_specs.append(head_spec(qb + 3 * per))
    tab_spec = pl.BlockSpec((s, HEAD_DIM), lambda i, h: (0, 0))
    in_specs += [tab_spec, tab_spec]
    return pl.pallas_call(
        _attn_kernel,
        grid=(b, n_heads),
        in_specs=in_specs,
        out_specs=pl.BlockSpec((None, s, HEAD_DIM), lambda i, h: (i, 0, h)),
        out_shape=jax.ShapeDtypeStruct((b, s, n_heads * HEAD_DIM), BF16),
        scratch_shapes=[pltpu.VMEM((2 * sum(d == 1 for d in DILATIONS), s, HEAD_DIM), BF16),
                        pltpu.VMEM((3 * sum(d > 1 for d in DILATIONS), s, HEAD_DIM), F32),
                        pltpu.VMEM((2, SPAN, 2 * SPAN), F32),
                        pltpu.VMEM((ATT_MAX_BATCH * SPAN, 2 * SPAN), F32),
                        pltpu.VMEM((ATT_MAX_BATCH * SPAN, HEAD_DIM), F32),
                        pltpu.VMEM((ATT_GROUPS, s, HEAD_DIM), F32),
                        pltpu.VMEM((ATT_GROUPS, s, HEAD_DIM), F32),
                        pltpu.VMEM((ATT_GROUPS, s, HEAD_DIM), F32)],
        compiler_params=_params(("parallel", "parallel")),
        name="dilated_attention",
    )(*([z3] * 10), cos_t, sin_t)


OUT_SUB_ROWS = 128


def _out_proj_kernel(*refs, with_next, conv_in):
    refs = list(refs)
    ya_ref = refs.pop(0)
    if conv_in:
        conv_ref, gate_ref, ng_ref, nb_ref = refs[:4]
        del refs[:4]
    else:
        yb_ref = refs.pop(0)
    w_ref, x_ref, gpost_ref = refs[:3]
    del refs[:3]
    if with_next:
        gnext_ref, xo_ref, ho_ref, wb_ref = refs
    else:
        xo_ref, wb_ref = refs

    @pl.when(pl.program_id(0) == 0)
    def _():
        _cast_weight(w_ref, wb_ref)

    half = ya_ref.shape[1]
    for r in range(0, ya_ref.shape[0], OUT_SUB_ROWS):
        rows = slice(r, r + OUT_SUB_ROWS)
        if conv_in:
            c = conv_ref[rows, :]
            mu = jnp.mean(c, axis=-1, keepdims=True)
            cc = c - mu
            var = jnp.mean(cc * cc, axis=-1, keepdims=True)
            yb = _silu(cc * lax.rsqrt(var + EPS) * ng_ref[...] + nb_ref[...])
            yb = (yb * _silu(gate_ref[rows, :].astype(F32))).astype(BF16)
        else:
            yb = yb_ref[rows, :]
        y = jnp.dot(ya_ref[rows, :], wb_ref[0:half, :], preferred_element_type=F32)
        y = y + jnp.dot(yb, wb_ref[half:2 * half, :], preferred_element_type=F32)
        yn = y * lax.rsqrt(jnp.mean(y * y, axis=-1, keepdims=True) + EPS) * gpost_ref[...]
        xn = x_ref[rows, :] + yn
        xo_ref[rows, :] = xn
        if with_next:
            hn = xn * lax.rsqrt(jnp.mean(xn * xn, axis=-1, keepdims=True) + EPS) * gnext_ref[...]
            ho_ref[rows, :] = hn.astype(ho_ref.dtype)


def _out_proj(ya, yb, w_all, layer, x, g_post, g_next=None, tm=512):
    t, half = ya.shape
    _, k, d = w_all.shape
    with_next = g_next is not None
    conv_in = isinstance(yb, tuple)
    row = lambda i: (i, 0)
    fixed = lambda i: (0, 0)
    in_specs = [pl.BlockSpec((tm, half), row)]
    args = [ya]
    if conv_in:
        conv, z, gate_blk, cn_g, cn_b = yb
        in_specs += [pl.BlockSpec((tm, half), row),
                     pl.BlockSpec((tm, half), lambda i: (i, gate_blk)),
                     pl.BlockSpec((1, half), fixed), pl.BlockSpec((1, half), fixed)]
        args += [conv, z, cn_g.reshape(1, half), cn_b.reshape(1, half)]
    else:
        in_specs.append(pl.BlockSpec((tm, half), row))
        args.append(yb)
    in_specs += [pl.BlockSpec((None, k, d), lambda i: (layer, 0, 0), pipeline_mode=pl.Buffered(1)),
                 pl.BlockSpec((tm, d), row),
                 pl.BlockSpec((1, d), fixed)]
    args += [w_all, x, g_post.reshape(1, d)]
    out_specs = [pl.BlockSpec((tm, d), row)]
    out_shape = [jax.ShapeDtypeStruct((t, d), F32)]
    if with_next:
        in_specs.append(pl.BlockSpec((1, d), fixed))
        args.append(g_next.reshape(1, d))
        out_specs.append(pl.BlockSpec((tm, d), row))
        out_shape.append(jax.ShapeDtypeStruct((t, d), BF16))
    return pl.pallas_call(
        functools.partial(_out_proj_kernel, with_next=with_next, conv_in=conv_in),
        grid=(t // tm,),
        in_specs=in_specs,
        out_specs=out_specs,
        out_shape=out_shape,
        scratch_shapes=[pltpu.VMEM((k, d), BF16)],
        compiler_params=_params(("arbitrary",)),
        name="out_proj",
    )(*args)


def _sgu_kernel(u_ref, v_ref, cg_ref, g_ref, b_ref, ws_ref, bias_ref, o_ref):
    tm = u_ref.shape[0]
    ii = lax.broadcasted_iota(jnp.int32, (CHUNK, CHUNK), 0)
    jj = lax.broadcasted_iota(jnp.int32, (CHUNK, CHUNK), 1)
    tril = jj <= ii
    for c in range(tm // CHUNK):
        rows = slice(c * CHUNK, (c + 1) * CHUNK)
        v = v_ref[rows, :].astype(F32)
        mu = jnp.mean(v, axis=-1, keepdims=True)
        vc = v - mu
        var = jnp.mean(vc * vc, axis=-1, keepdims=True)
        vn = (vc * lax.rsqrt(var + EPS) * g_ref[...] + b_ref[...]).astype(BF16)
        for h in range(SGU_GROUPS):
            cs = slice(h * SGU_CH, (h + 1) * SGU_CH)
            wm = jnp.where(tril, ws_ref[h], 0.0).astype(BF16)
            sg = jnp.dot(wm, vn[:, cs], preferred_element_type=F32) + bias_ref[:, cs]
            y = u_ref[rows, cs].astype(F32) * sg * _silu(cg_ref[rows, cs].astype(F32))
            o_ref[rows, cs] = y.astype(o_ref.dtype)


def _sgu(z, sgu_g, sgu_b, sgu_w, sgu_bias, tm=512):
    t = z.shape[0]
    width = SGU_GROUPS * SGU_CH
    bias_full = jnp.repeat(sgu_bias.T, SGU_CH, axis=1)
    fixed2 = lambda i: (0, 0)
    return pl.pallas_call(
        _sgu_kernel,
        grid=(t // tm,),
        in_specs=[pl.BlockSpec((tm, width), lambda i: (i, 0)),
                  pl.BlockSpec((tm, width), lambda i: (i, 1)),
                  pl.BlockSpec((tm, width), lambda i: (i, 2)),
                  pl.BlockSpec((1, width), fixed2), pl.BlockSpec((1, width), fixed2),
                  pl.BlockSpec(sgu_w.shape, lambda i: (0, 0, 0)),
                  pl.BlockSpec((CHUNK, width), fixed2)],
        out_specs=pl.BlockSpec((tm, width), lambda i: (i, 0)),
        out_shape=jax.ShapeDtypeStruct((t, width), BF16),
        compiler_params=_params(("parallel",)),
        name="sgu",
    )(z, z, z, sgu_g.reshape(1, width), sgu_b.reshape(1, width), sgu_w, bias_full)


CONV_PAD = 32
CONV_ROWS = 128
CONV_FILL = 128


def _conv_kernel(dv_ref, dg_ref, w_ref, cb_ref, o_ref, dpad_ref):
    s_len = dv_ref.shape[0]
    dpad_ref[0:CONV_PAD, :] = jnp.zeros((CONV_PAD, LANES), F32)

    def fill(i, carry):
        r = pl.multiple_of(i * CONV_FILL, CONV_FILL)
        rows = pl.ds(r, CONV_FILL)
        d = dv_ref[rows, :].astype(F32) * jax.nn.sigmoid(dg_ref[rows, :].astype(F32))
        dpad_ref[pl.ds(CONV_PAD + r, CONV_FILL), :] = d
        return carry

    lax.fori_loop(0, s_len // CONV_FILL, fill, 0)

    lead = CONV_PAD - (CONV_K - 1)
    taps = [w_ref[k:k + 1, :] for k in range(CONV_K)]
    for r in range(0, s_len, CONV_ROWS):
        acc = dpad_ref[r + lead:r + lead + CONV_ROWS, :] * taps[0]
        for k in range(1, CONV_K):
            acc = acc + dpad_ref[r + lead + k:r + lead + k + CONV_ROWS, :] * taps[k]
        o_ref[r:r + CONV_ROWS, :] = acc + cb_ref[...]


def _conv_taps(z3, col, conv_w, conv_b):
    b, s, cols = z3.shape
    ch = conv_w.shape[1]
    n_slabs = ch // LANES
    blk = col // LANES
    vec = lambda a: a.reshape(1, ch)
    return pl.pallas_call(
        _conv_kernel,
        grid=(b, n_slabs),
        in_specs=[pl.BlockSpec((None, s, LANES), lambda i, c: (i, 0, blk + c)),
                  pl.BlockSpec((None, s, LANES), lambda i, c: (i, 0, blk + n_slabs + c)),
                  pl.BlockSpec((CONV_K, LANES), lambda i, c: (0, c)),
                  pl.BlockSpec((1, LANES), lambda i, c: (0, c))],
        out_specs=pl.BlockSpec((None, s, LANES), lambda i, c: (i, 0, c)),
        out_shape=jax.ShapeDtypeStruct((b, s, ch), F32),
        scratch_shapes=[pltpu.VMEM((CONV_PAD + s, LANES), F32)],
        compiler_params=_params(("parallel", "parallel")),
        name="conv_taps",
    )(z3, z3, conv_w, vec(conv_b))


def kernel(x, e_pre_norm, e_w_in, e_pool_w, e_pool_scale, e_w_out, e_post_norm, o_pre_norm, o_w_in, o_sgu_norm_g, o_sgu_norm_b, o_sgu_w, o_sgu_b, o_conv_w, o_conv_b, o_conv_norm_g, o_conv_norm_b, o_w_out, o_post_norm):
    b, s, d = x.shape
    t = b * s
    n_even, n_odd = e_w_in.shape[0], o_w_in.shape[0]
    depth = n_even + n_odd
    pool_width = len(POOL_WINDOWS) * POOL_CH
    n_heads = (e_w_out.shape[1] - pool_width) // HEAD_DIM
    sgu_width = SGU_GROUPS * SGU_CH

    def pre_gain(i):
        return e_pre_norm[i // 2] if i % 2 == 0 else o_pre_norm[i // 2]

    xf = x.reshape(t, d)
    h = _rmsnorm(xf, pre_gain(0))
    for i in range(depth):
        j = i // 2
        g_next = pre_gain(i + 1) if i + 1 < depth else None
        if i % 2 == 0:
            z = _in_proj(h, e_w_in, j)
            z3 = z.reshape(b, s, z.shape[1])
            ya = _pool_mixer(z3, e_pool_w[j], e_pool_scale[j])
            yb = _dilated_attention(z3, 2 * pool_width, n_heads)
            w_out, g_post = e_w_out, e_post_norm[j]
        else:
            z = _in_proj(h, o_w_in, j)
            z3 = z.reshape(b, s, z.shape[1])
            ya = _sgu(z, o_sgu_norm_g[j], o_sgu_norm_b[j], o_sgu_w[j], o_sgu_b[j])
            conv_width = o_conv_w.shape[2]
            conv = _conv_taps(z3, 3 * sgu_width, o_conv_w[j], o_conv_b[j])
            gate_blk = (3 * sgu_width + 2 * conv_width) // conv_width
            yb = (conv.reshape(t, conv_width), z, gate_blk, o_conv_norm_g[j], o_conv_norm_b[j])
            w_out, g_post = o_w_out, o_post_norm[j]
        if not isinstance(yb, tuple):
            yb = yb.reshape(t, -1)
        outs = _out_proj(ya.reshape(t, -1), yb, w_out, j, xf, g_post, g_next)
        if g_next is None:
            xf = outs[0]
        else:
            xf, h = outs
    return xf.reshape(b, s, d)
```

```python
import functools

import jax
import jax.numpy as jnp
import numpy as np
from jax import lax
from jax.experimental import pallas as pl
from jax.experimental.pallas import tpu as pltpu

F32 = jnp.float32
BF16 = jnp.bfloat16

EPS = 1e-6
NEG = -1e30
HEAD_DIM = 128
ROT_DIM = HEAD_DIM // 4
ROT_HALF = ROT_DIM // 2
ROPE_THETA = 500000.0
DILATIONS = (1, 4, 16)
SPAN = 128
ATT_GROUPS = len(DILATIONS)
POOL_WINDOWS = (2, 4, 8, 16)
POOL_CH = 256
SGU_GROUPS = 4
SGU_CH = 256
CHUNK = 128
CONV_K = 31

LANES = 128
VMEM_LIMIT = 56 * 1024 * 1024


def _params(sem, vmem=VMEM_LIMIT):
    return pltpu.CompilerParams(dimension_semantics=sem, vmem_limit_bytes=vmem)


def _silu(x):
    return x * jax.nn.sigmoid(x)


def _rmsnorm_kernel(x_ref, g_ref, o_ref):
    x = x_ref[...]
    ms = jnp.mean(x * x, axis=-1, keepdims=True)
    o_ref[...] = (x * lax.rsqrt(ms + EPS) * g_ref[...]).astype(o_ref.dtype)


def _rmsnorm(x, g, tm=1024):
    t, d = x.shape
    return pl.pallas_call(
        _rmsnorm_kernel,
        grid=(t // tm,),
        in_specs=[pl.BlockSpec((tm, d), lambda i: (i, 0)),
                  pl.BlockSpec((1, d), lambda i: (0, 0))],
        out_specs=pl.BlockSpec((tm, d), lambda i: (i, 0)),
        out_shape=jax.ShapeDtypeStruct((t, d), BF16),
        compiler_params=_params(("parallel",)),
        name="rmsnorm",
    )(x, g.reshape(1, d))


CAST_ROWS = 256


def _cast_weight(w_ref, wb_ref):
    def chunk(i, carry):
        rows = pl.ds(pl.multiple_of(i * CAST_ROWS, CAST_ROWS), CAST_ROWS)
        wb_ref[rows, :] = w_ref[rows, :].astype(wb_ref.dtype)
        return carry

    lax.fori_loop(0, w_ref.shape[0] // CAST_ROWS, chunk, 0)


def _in_proj_kernel(a_ref, w_ref, o_ref, wb_ref):
    first = pl.program_id(1) == 0

    @pl.when(first)
    def _():
        acc = None
        for c in range(0, w_ref.shape[0], CAST_ROWS):
            wc = w_ref[c:c + CAST_ROWS, :].astype(BF16)
            wb_ref[c:c + CAST_ROWS, :] = wc
            part = jnp.dot(a_ref[:, c:c + CAST_ROWS], wc, preferred_element_type=F32)
            acc = part if acc is None else acc + part
        o_ref[...] = acc.astype(o_ref.dtype)

    @pl.when(jnp.logical_not(first))
    def _():
        o_ref[...] = jnp.dot(a_ref[...], wb_ref[...],
                             preferred_element_type=F32).astype(o_ref.dtype)


def _in_proj(a, w_all, layer, tm=2048, tn=1024):
    m, k = a.shape
    n = w_all.shape[2]
    return pl.pallas_call(
        _in_proj_kernel,
        grid=(n // tn, m // tm),
        in_specs=[pl.BlockSpec((tm, k), lambda j, i: (i, 0)),
                  pl.BlockSpec((None, k, tn), lambda j, i: (layer, 0, j))],
        out_specs=pl.BlockSpec((tm, tn), lambda j, i: (i, j)),
        out_shape=jax.ShapeDtypeStruct((m, n), BF16),
        scratch_shapes=[pltpu.VMEM((k, tn), BF16)],
        compiler_params=_params(("parallel", "arbitrary")),
        name="in_proj",
    )(a, w_all)


POOL_PAD = 16
POOL_ROWS = 128
POOL_FILL = 256


def _pool_kernel(z_ref, pw_ref, ps_ref, o_ref, xpad_ref, pwb_ref):
    s_len = z_ref.shape[0]
    width = len(POOL_WINDOWS) * POOL_CH
    n_slabs = width // LANES
    slabs_per_group = POOL_CH // LANES
    for g in range(len(POOL_WINDOWS)):
        pwb_ref[g] = pw_ref[g].astype(BF16)
    for c in range(n_slabs):
        xpad_ref[c, 0:POOL_PAD, :] = jnp.zeros((POOL_PAD, LANES), F32)

    def fill(i, carry):
        r = pl.multiple_of(i * POOL_FILL, POOL_FILL)
        for c in range(n_slabs):
            xpad_ref[c, pl.ds(POOL_PAD + r, POOL_FILL), :] = (
                z_ref[pl.ds(r, POOL_FILL), c * LANES:(c + 1) * LANES].astype(F32))
        return carry

    lax.fori_loop(0, s_len // POOL_FILL, fill, 0)

    for r in range(0, s_len, POOL_ROWS):
        for g, w in enumerate(POOL_WINDOWS):
            cs = slice(g * POOL_CH, (g + 1) * POOL_CH)
            xs, accs = [], []
            for c in range(g * slabs_per_group, (g + 1) * slabs_per_group):
                lo = POOL_PAD + r
                xc = xpad_ref[c, lo:lo + POOL_ROWS, :]
                ac = xc
                for j in range(1, w):
                    ac = ac + xpad_ref[c, lo - j:lo - j + POOL_ROWS, :]
                xs.append(xc)
                accs.append(ac)
            x = jnp.concatenate(xs, axis=-1)
            acc = jnp.concatenate(accs, axis=-1)
            if r + 1 >= w:
                cnt = float(w)
            else:
                row = r + lax.broadcasted_iota(jnp.int32, (POOL_ROWS, POOL_CH), 0)
                cnt = jnp.minimum(row + 1, w).astype(F32)
            pooled = acc / cnt - x
            mixed = jnp.dot(pooled.astype(BF16), pwb_ref[g], preferred_element_type=F32)
            gate = z_ref[r:r + POOL_ROWS, width + g * POOL_CH:width + (g + 1) * POOL_CH].astype(F32)
            y = mixed * ps_ref[:, cs] * _silu(gate)
            o_ref[r:r + POOL_ROWS, cs] = y.astype(o_ref.dtype)


def _pool_mixer(z3, pool_w, pool_scale):
    b, s, _ = z3.shape
    width = len(POOL_WINDOWS) * POOL_CH
    return pl.pallas_call(
        _pool_kernel,
        grid=(b,),
        in_specs=[pl.BlockSpec((None, s, 2 * width), lambda i: (i, 0, 0)),
                  pl.BlockSpec(pool_w.shape, lambda i: (0, 0, 0)),
                  pl.BlockSpec((1, width), lambda i: (0, 0))],
        out_specs=pl.BlockSpec((None, s, width), lambda i: (i, 0, 0)),
        out_shape=jax.ShapeDtypeStruct((b, s, width), BF16),
        scratch_shapes=[pltpu.VMEM((width // LANES, POOL_PAD + s, LANES), F32),
                        pltpu.VMEM(pool_w.shape, BF16)],
        compiler_params=_params(("parallel",)),
        name="pool_mixer",
    )(z3, pool_w, pool_scale.reshape(1, width))


ATT_SCALE_LOG2 = HEAD_DIM ** -0.5 * float(np.log2(np.e))
ATT_ROWS = 1024
ATT_PREP_ROWS = 2048
ATT_MAX_BATCH = 16


def _for_chunks(n, body):
    if n == 1:
        body(0, 0)
    else:
        lax.fori_loop(0, n, body, 0)


def _aligned(start, align):
    return start if isinstance(start, int) else pl.multiple_of(start, align)


def _attn_kernel(q0, k0, v0, q1, k1, v1, q2, k2, v2, gate_ref, cos_ref, sin_ref,
                 o_ref, qkd_ref, qkv_ref, bias_ref, s_ref, ms_ref, acc_ref, m_ref, l_ref):
    s_len = o_ref.shape[0]
    qkv_in = ((q0, k0, v0), (q1, k1, v1), (q2, k2, v2))

    src = lax.broadcasted_iota(jnp.int32, (HEAD_DIM, HEAD_DIM), 0)
    dst = lax.broadcasted_iota(jnp.int32, (HEAD_DIM, HEAD_DIM), 1)
    perm = (jnp.where((dst < ROT_HALF) & (src == dst + ROT_HALF), -1.0, 0.0)
            + jnp.where((dst >= ROT_HALF) & (dst < ROT_DIM) & (src == dst - ROT_HALF), 1.0, 0.0)
            ).astype(BF16)

    strided = [g for g, d in enumerate(DILATIONS) if d > 1]
    dense = [g for g, d in enumerate(DILATIONS) if d == 1]

    prep_rows, merge_rows = min(ATT_PREP_ROWS, s_len), min(ATT_ROWS, s_len)

    def prep(i, c):
        chunk = pl.ds(_aligned(i * prep_rows, prep_rows), prep_rows)
        cos, sin = cos_ref[chunk, :], sin_ref[chunk, :]
        for g, d in enumerate(DILATIONS):
            for j in range(2):
                t = qkv_in[g][j][chunk, :]
                partner = jnp.dot(t, perm, preferred_element_type=F32)
                roped = t.astype(F32) * cos + partner * sin
                if d == 1:
                    qkd_ref[2 * dense.index(g) + j, chunk, :] = roped.astype(BF16)
                else:
                    qkv_ref[3 * strided.index(g) + j, chunk, :] = roped
            if d > 1:
                qkv_ref[3 * strided.index(g) + 2, chunk, :] = qkv_in[g][2][chunk, :].astype(F32)
        return c

    _for_chunks(s_len // prep_rows, prep)

    qi = lax.broadcasted_iota(jnp.int32, (SPAN, 2 * SPAN), 0)
    kj = lax.broadcasted_iota(jnp.int32, (SPAN, 2 * SPAN), 1)
    band, causal = 0, 1
    bias_ref[band] = jnp.where((kj >= qi) & (kj <= qi + SPAN), 0.0, NEG)
    bias_ref[causal] = jnp.where(kj <= qi, 0.0, NEG)

    def rows(d, start, cnt):
        return pl.ds(_aligned(start, SPAN), cnt) if d == 1 else pl.ds(start, cnt, stride=d)

    def operand(g, d, j, start, cnt):
        if d == 1:
            src = qkv_in[g][2] if j == 2 else qkd_ref.at[2 * dense.index(g) + j]
            return src[rows(d, start, cnt), :]
        return qkv_ref[3 * strided.index(g) + j, rows(d, start, cnt), :].astype(BF16)

    def run_batch(blocks, n_keys, mask):
        assert len(blocks) <= ATT_MAX_BATCH

        def blk(j):
            return slice(j * SPAN, (j + 1) * SPAN)

        for j, (g, d, q_start, k_start) in enumerate(blocks):
            s = lax.dot_general(operand(g, d, 0, q_start, SPAN), operand(g, d, 1, k_start, n_keys),
                                (((1,), (1,)), ((), ())), preferred_element_type=F32)
            s = s * ATT_SCALE_LOG2 + bias_ref[mask, :, 0:n_keys]
            s_ref[blk(j), 0:n_keys] = s
            m = jnp.max(s, axis=-1, keepdims=True)
            ms_ref[blk(j), :] = jnp.broadcast_to(m, (SPAN, HEAD_DIM))

        for j, (g, d, q_start, k_start) in enumerate(blocks):
            m = ms_ref[blk(j), :]
            den, ps = None, []
            for c in range(n_keys // HEAD_DIM):
                p = jnp.exp2(s_ref[blk(j), c * HEAD_DIM:(c + 1) * HEAD_DIM] - m)
                ps.append(p.astype(BF16))
                den = p if den is None else den + p
            den = jnp.sum(den, axis=-1, keepdims=True)
            m_ref[g, rows(d, q_start, SPAN), :] = m
            l_ref[g, rows(d, q_start, SPAN), :] = jnp.broadcast_to(den, (SPAN, HEAD_DIM))
            pb = ps[0] if len(ps) == 1 else jnp.concatenate(ps, axis=1)
            acc_ref[g, rows(d, q_start, SPAN), :] = jnp.dot(
                pb, operand(g, d, 2, k_start, n_keys), preferred_element_type=F32)

    first_blocks = [(g, d, r, r) for g, d in enumerate(DILATIONS) for r in range(d)]
    for lo in range(0, len(first_blocks), ATT_MAX_BATCH):
        run_batch(first_blocks[lo:lo + ATT_MAX_BATCH], SPAN, causal)
    for g, d in enumerate(DILATIONS):
        stride_blk = d * SPAN
        band_blocks = [(g, d, r + i * stride_blk, r + (i - 1) * stride_blk)
                       for i in range(1, s_len // stride_blk) for r in range(d)]
        for lo in range(0, len(band_blocks), ATT_MAX_BATCH):
            run_batch(band_blocks[lo:lo + ATT_MAX_BATCH], 2 * SPAN, band)

    def merge(i, c):
        rows = pl.ds(_aligned(i * merge_rows, merge_rows), merge_rows)
        ms = [m_ref[g, rows, :] for g in range(ATT_GROUPS)]
        m = functools.reduce(jnp.maximum, ms)
        num = den = None
        for g in range(ATT_GROUPS):
            e = jnp.exp2(ms[g] - m)
            ng, dg = e * acc_ref[g, rows, :], e * l_ref[g, rows, :]
            num, den = (ng, dg) if num is None else (num + ng, den + dg)
        gate = gate_ref[rows, :].astype(F32)
        o_ref[rows, :] = (num / den * _silu(gate)).astype(o_ref.dtype)
        return c

    _for_chunks(s_len // merge_rows, merge)


def _rope_tables(s_len):
    pos = jnp.arange(s_len, dtype=F32)
    inv_freq = jnp.power(ROPE_THETA, -jnp.arange(0, ROT_DIM, 2, dtype=F32) / ROT_DIM)
    ang = pos[:, None] * inv_freq[None, :]
    cos, sin = jnp.cos(ang), jnp.sin(ang)
    rest = jnp.zeros((s_len, HEAD_DIM - ROT_DIM), F32)
    return (jnp.concatenate([cos, cos, rest + 1.0], axis=-1),
            jnp.concatenate([sin, sin, rest], axis=-1))


def _dilated_attention(z3, q_col, n_heads):
    b, s, _ = z3.shape
    per = ATT_GROUPS * n_heads
    qb = q_col // HEAD_DIM
    cos_t, sin_t = _rope_tables(s)

    def head_spec(blk0):
        return pl.BlockSpec((None, s, HEAD_DIM), lambda i, h, blk0=blk0: (i, 0, blk0 + h))

    in_specs = []
    for g in range(ATT_GROUPS):
        for j in range(3):
            in_specs.append(head_spec(qb + j * per + g * n_heads))
    in_specs.append(head_spec(qb + 3 * per))
    tab_spec = pl.BlockSpec((s, HEAD_DIM), lambda i, h: (0, 0))
    in_specs += [tab_spec, tab_spec]
    return pl.pallas_call(
        _attn_kernel,
        grid=(b, n_heads),
        in_specs=in_specs,
        out_specs=pl.BlockSpec((None, s, HEAD_DIM), lambda i, h: (i, 0, h)),
        out_shape=jax.ShapeDtypeStruct((b, s, n_heads * HEAD_DIM), BF16),
        scratch_shapes=[pltpu.VMEM((2 * sum(d == 1 for d in DILATIONS), s, HEAD_DIM), BF16),
                        pltpu.VMEM((3 * sum(d > 1 for d in DILATIONS), s, HEAD_DIM), F32),
                        pltpu.VMEM((2, SPAN, 2 * SPAN), F32),
                        pltpu.VMEM((ATT_MAX_BATCH * SPAN, 2 * SPAN), F32),
                        pltpu.VMEM((ATT_MAX_BATCH * SPAN, HEAD_DIM), F32),
                        pltpu.VMEM((ATT_GROUPS, s, HEAD_DIM), F32),
                        pltpu.VMEM((ATT_GROUPS, s, HEAD_DIM), F32),
                        pltpu.VMEM((ATT_GROUPS, s, HEAD_DIM), F32)],
        compiler_params=_params(("parallel", "parallel")),
        name="dilated_attention",
    )(*([z3] * 10), cos_t, sin_t)


OUT_SUB_ROWS = 128


def _out_proj_kernel(*refs, with_next, conv_in):
    refs = list(refs)
    ya_ref = refs.pop(0)
    if conv_in:
        conv_ref, gate_ref, ng_ref, nb_ref = refs[:4]
        del refs[:4]
    else:
        yb_ref = refs.pop(0)
    w_ref, x_ref, gpost_ref = refs[:3]
    del refs[:3]
    if with_next:
        gnext_ref, xo_ref, ho_ref, wb_ref = refs
    else:
        xo_ref, wb_ref = refs

    @pl.when(pl.program_id(0) == 0)
    def _():
        _cast_weight(w_ref, wb_ref)

    half = ya_ref.shape[1]
    for r in range(0, ya_ref.shape[0], OUT_SUB_ROWS):
        rows = slice(r, r + OUT_SUB_ROWS)
        if conv_in:
            c = conv_ref[rows, :]
            mu = jnp.mean(c, axis=-1, keepdims=True)
            cc = c - mu
            var = jnp.mean(cc * cc, axis=-1, keepdims=True)
            yb = _silu(cc * lax.rsqrt(var + EPS) * ng_ref[...] + nb_ref[...])
            yb = (yb * _silu(gate_ref[rows, :].astype(F32))).astype(BF16)
        else:
            yb = yb_ref[rows, :]
        y = jnp.dot(ya_ref[rows, :], wb_ref[0:half, :], preferred_element_type=F32)
        y = y + jnp.dot(yb, wb_ref[half:2 * half, :], preferred_element_type=F32)
        yn = y * lax.rsqrt(jnp.mean(y * y, axis=-1, keepdims=True) + EPS) * gpost_ref[...]
        xn = x_ref[rows, :] + yn
        xo_ref[rows, :] = xn
        if with_next:
            hn = xn * lax.rsqrt(jnp.mean(xn * xn, axis=-1, keepdims=True) + EPS) * gnext_ref[...]
            ho_ref[rows, :] = hn.astype(ho_ref.dtype)


def _out_proj(ya, yb, w_all, layer, x, g_post, g_next=None, tm=512):
    t, half = ya.shape
    _, k, d = w_all.shape
    with_next = g_next is not None
    conv_in = isinstance(yb, tuple)
    row = lambda i: (i, 0)
    fixed = lambda i: (0, 0)
    in_specs = [pl.BlockSpec((tm, half), row)]
    args = [ya]
    if conv_in:
        conv, z, gate_blk, cn_g, cn_b = yb
        in_specs += [pl.BlockSpec((tm, half), row),
                     pl.BlockSpec((tm, half), lambda i: (i, gate_blk)),
                     pl.BlockSpec((1, half), fixed), pl.BlockSpec((1, half), fixed)]
        args += [conv, z, cn_g.reshape(1, half), cn_b.reshape(1, half)]
    else:
        in_specs.append(pl.BlockSpec((tm, half), row))
        args.append(yb)
    in_specs += [pl.BlockSpec((None, k, d), lambda i: (layer, 0, 0), pipeline_mode=pl.Buffered(1)),
                 pl.BlockSpec((tm, d), row),
                 pl.BlockSpec((1, d), fixed)]
    args += [w_all, x, g_post.reshape(1, d)]
    out_specs = [pl.BlockSpec((tm, d), row)]
    out_shape = [jax.ShapeDtypeStruct((t, d), F32)]
    if with_next:
        in_specs.append(pl.BlockSpec((1, d), fixed))
        args.append(g_next.reshape(1, d))
        out_specs.append(pl.BlockSpec((tm, d), row))
        out_shape.append(jax.ShapeDtypeStruct((t, d), BF16))
    return pl.pallas_call(
        functools.partial(_out_proj_kernel, with_next=with_next, conv_in=conv_in),
        grid=(t // tm,),
        in_specs=in_specs,
        out_specs=out_specs,
        out_shape=out_shape,
        scratch_shapes=[pltpu.VMEM((k, d), BF16)],
        compiler_params=_params(("arbitrary",)),
        name="out_proj",
    )(*args)


def _sgu_kernel(u_ref, v_ref, cg_ref, g_ref, b_ref, ws_ref, bias_ref, o_ref):
    tm = u_ref.shape[0]
    ii = lax.broadcasted_iota(jnp.int32, (CHUNK, CHUNK), 0)
    jj = lax.broadcasted_iota(jnp.int32, (CHUNK, CHUNK), 1)
    tril = jj <= ii
    for c in range(tm // CHUNK):
        rows = slice(c * CHUNK, (c + 1) * CHUNK)
        v = v_ref[rows, :].astype(F32)
        mu = jnp.mean(v, axis=-1, keepdims=True)
        vc = v - mu
        var = jnp.mean(vc * vc, axis=-1, keepdims=True)
        vn = (vc * lax.rsqrt(var + EPS) * g_ref[...] + b_ref[...]).astype(BF16)
        for h in range(SGU_GROUPS):
            cs = slice(h * SGU_CH, (h + 1) * SGU_CH)
            wm = jnp.where(tril, ws_ref[h], 0.0).astype(BF16)
            sg = jnp.dot(wm, vn[:, cs], preferred_element_type=F32) + bias_ref[:, cs]
            y = u_ref[rows, cs].astype(F32) * sg * _silu(cg_ref[rows, cs].astype(F32))
            o_ref[rows, cs] = y.astype(o_ref.dtype)


def _sgu(z, sgu_g, sgu_b, sgu_w, sgu_bias, tm=1024):
    t = z.shape[0]
    width = SGU_GROUPS * SGU_CH
    bias_full = jnp.repeat(sgu_bias.T, SGU_CH, axis=1)
    fixed2 = lambda i: (0, 0)
    return pl.pallas_call(
        _sgu_kernel,
        grid=(t // tm,),
        in_specs=[pl.BlockSpec((tm, width), lambda i: (i, 0)),
                  pl.BlockSpec((tm, width), lambda i: (i, 1)),
                  pl.BlockSpec((tm, width), lambda i: (i, 2)),
                  pl.BlockSpec((1, width), fixed2), pl.BlockSpec((1, width), fixed2),
                  pl.BlockSpec(sgu_w.shape, lambda i: (0, 0, 0)),
                  pl.BlockSpec((CHUNK, width), fixed2)],
        out_specs=pl.BlockSpec((tm, width), lambda i: (i, 0)),
        out_shape=jax.ShapeDtypeStruct((t, width), BF16),
        compiler_params=_params(("parallel",)),
        name="sgu",
    )(z, z, z, sgu_g.reshape(1, width), sgu_b.reshape(1, width), sgu_w, bias_full)


CONV_PAD = 32
CONV_ROWS = 128
CONV_FILL = 128
CONV_SLABS = 2


def _conv_kernel(dv_ref, dg_ref, w_ref, cb_ref, o_ref, dpad_ref):
    s_len = dv_ref.shape[0]
    for c in range(CONV_SLABS):
        dpad_ref[c, 0:CONV_PAD, :] = jnp.zeros((CONV_PAD, LANES), F32)

    def fill(i, carry):
        r = pl.multiple_of(i * CONV_FILL, CONV_FILL)
        rows = pl.ds(r, CONV_FILL)
        d = dv_ref[rows, :].astype(F32) * jax.nn.sigmoid(dg_ref[rows, :].astype(F32))
        for c in range(CONV_SLABS):
            dpad_ref[c, pl.ds(CONV_PAD + r, CONV_FILL), :] = d[:, c * LANES:(c + 1) * LANES]
        return carry

    lax.fori_loop(0, s_len // CONV_FILL, fill, 0)

    lead = CONV_PAD - (CONV_K - 1)
    for c in range(CONV_SLABS):
        cs = slice(c * LANES, (c + 1) * LANES)
        taps = [w_ref[k:k + 1, cs] for k in range(CONV_K)]
        for r in range(0, s_len, CONV_ROWS):
            acc = dpad_ref[c, r + lead:r + lead + CONV_ROWS, :] * taps[0]
            for k in range(1, CONV_K):
                acc = acc + dpad_ref[c, r + lead + k:r + lead + k + CONV_ROWS, :] * taps[k]
            o_ref[r:r + CONV_ROWS, cs] = acc + cb_ref[:, cs]


def _conv_taps(z3, col, conv_w, conv_b):
    b, s, cols = z3.shape
    ch = conv_w.shape[1]
    width = CONV_SLABS * LANES
    n_steps = ch // width
    blk = col // width
    vec = lambda a: a.reshape(1, ch)
    return pl.pallas_call(
        _conv_kernel,
        grid=(b, n_steps),
        in_specs=[pl.BlockSpec((None, s, width), lambda i, c: (i, 0, blk + c)),
                  pl.BlockSpec((None, s, width), lambda i, c: (i, 0, blk + n_steps + c)),
                  pl.BlockSpec((CONV_K, width), lambda i, c: (0, c)),
                  pl.BlockSpec((1, width), lambda i, c: (0, c))],
        out_specs=pl.BlockSpec((None, s, width), lambda i, c: (i, 0, c)),
        out_shape=jax.ShapeDtypeStruct((b, s, ch), F32),
        scratch_shapes=[pltpu.VMEM((CONV_SLABS, CONV_PAD + s, LANES), F32)],
        compiler_params=_params(("parallel", "parallel")),
        name="conv_taps",
    )(z3, z3, conv_w, vec(conv_b))


def kernel(x, e_pre_norm, e_w_in, e_pool_w, e_pool_scale, e_w_out, e_post_norm, o_pre_norm, o_w_in, o_sgu_norm_g, o_sgu_norm_b, o_sgu_w, o_sgu_b, o_conv_w, o_conv_b, o_conv_norm_g, o_conv_norm_b, o_w_out, o_post_norm):
    b, s, d = x.shape
    t = b * s
    n_even, n_odd = e_w_in.shape[0], o_w_in.shape[0]
    depth = n_even + n_odd
    pool_width = len(POOL_WINDOWS) * POOL_CH
    n_heads = (e_w_out.shape[1] - pool_width) // HEAD_DIM
    sgu_width = SGU_GROUPS * SGU_CH

    def pre_gain(i):
        return e_pre_norm[i // 2] if i % 2 == 0 else o_pre_norm[i // 2]

    xf = x.reshape(t, d)
    h = _rmsnorm(xf, pre_gain(0))
    for i in range(depth):
        j = i // 2
        g_next = pre_gain(i + 1) if i + 1 < depth else None
        if i % 2 == 0:
            z = _in_proj(h, e_w_in, j)
            z3 = z.reshape(b, s, z.shape[1])
            ya = _pool_mixer(z3, e_pool_w[j], e_pool_scale[j])
            yb = _dilated_attention(z3, 2 * pool_width, n_heads)
            w_out, g_post = e_w_out, e_post_norm[j]
        else:
            z = _in_proj(h, o_w_in, j)
            z3 = z.reshape(b, s, z.shape[1])
            ya = _sgu(z, o_sgu_norm_g[j], o_sgu_norm_b[j], o_sgu_w[j], o_sgu_b[j])
            conv_width = o_conv_w.shape[2]
            conv = _conv_taps(z3, 3 * sgu_width, o_conv_w[j], o_conv_b[j])
            gate_blk = (3 * sgu_width + 2 * conv_width) // conv_width
            yb = (conv.reshape(t, conv_width), z, gate_blk, o_conv_norm_g[j], o_conv_norm_b[j])
            w_out, g_post = o_w_out, o_post_norm[j]
        if not isinstance(yb, tuple):
            yb = yb.reshape(t, -1)
        outs = _out_proj(ya.reshape(t, -1), yb, w_out, j, xf, g_post, g_next)
        if g_next is None:
            xf = outs[0]
        else:
            xf, h = outs
    return xf.reshape(b, s, d)
```

```python
import functools

import jax
import jax.numpy as jnp
import numpy as np
from jax import lax
from jax.experimental import pallas as pl
from jax.experimental.pallas import tpu as pltpu

F32 = jnp.float32
BF16 = jnp.bfloat16

EPS = 1e-6
NEG = -1e30
HEAD_DIM = 128
ROT_DIM = HEAD_DIM // 4
ROT_HALF = ROT_DIM // 2
ROPE_THETA = 500000.0
DILATIONS = (1, 4, 16)
SPAN = 128
ATT_GROUPS = len(DILATIONS)
POOL_WINDOWS = (2, 4, 8, 16)
POOL_CH = 256
SGU_GROUPS = 4
SGU_CH = 256
CHUNK = 128
CONV_K = 31

LANES = 128
VMEM_LIMIT = 56 * 1024 * 1024


def _params(sem, vmem=VMEM_LIMIT):
    return pltpu.CompilerParams(dimension_semantics=sem, vmem_limit_bytes=vmem)


def _silu(x):
    return x * jax.nn.sigmoid(x)


def _rmsnorm_kernel(x_ref, g_ref, o_ref):
    x = x_ref[...]
    ms = jnp.mean(x * x, axis=-1, keepdims=True)
    o_ref[...] = (x * lax.rsqrt(ms + EPS) * g_ref[...]).astype(o_ref.dtype)


def _rmsnorm(x, g, tm=1024):
    t, d = x.shape
    return pl.pallas_call(
        _rmsnorm_kernel,
        grid=(t // tm,),
        in_specs=[pl.BlockSpec((tm, d), lambda i: (i, 0)),
                  pl.BlockSpec((1, d), lambda i: (0, 0))],
        out_specs=pl.BlockSpec((tm, d), lambda i: (i, 0)),
        out_shape=jax.ShapeDtypeStruct((t, d), BF16),
        compiler_params=_params(("parallel",)),
        name="rmsnorm",
    )(x, g.reshape(1, d))


CAST_ROWS = 256


def _cast_weight(w_ref, wb_ref):
    def chunk(i, carry):
        rows = pl.ds(pl.multiple_of(i * CAST_ROWS, CAST_ROWS), CAST_ROWS)
        wb_ref[rows, :] = w_ref[rows, :].astype(wb_ref.dtype)
        return carry

    lax.fori_loop(0, w_ref.shape[0] // CAST_ROWS, chunk, 0)


def _in_proj_kernel(a_ref, w_ref, o_ref, wb_ref):
    first = pl.program_id(1) == 0

    @pl.when(first)
    def _():
        acc = None
        for c in range(0, w_ref.shape[0], CAST_ROWS):
            wc = w_ref[c:c + CAST_ROWS, :].astype(BF16)
            wb_ref[c:c + CAST_ROWS, :] = wc
            part = jnp.dot(a_ref[:, c:c + CAST_ROWS], wc, preferred_element_type=F32)
            acc = part if acc is None else acc + part
        o_ref[...] = acc.astype(o_ref.dtype)

    @pl.when(jnp.logical_not(first))
    def _():
        o_ref[...] = jnp.dot(a_ref[...], wb_ref[...],
                             preferred_element_type=F32).astype(o_ref.dtype)


def _in_proj(a, w_all, layer, tm=2048, tn=1024):
    m, k = a.shape
    n = w_all.shape[2]
    return pl.pallas_call(
        _in_proj_kernel,
        grid=(n // tn, m // tm),
        in_specs=[pl.BlockSpec((tm, k), lambda j, i: (i, 0)),
                  pl.BlockSpec((None, k, tn), lambda j, i: (layer, 0, j))],
        out_specs=pl.BlockSpec((tm, tn), lambda j, i: (i, j)),
        out_shape=jax.ShapeDtypeStruct((m, n), BF16),
        scratch_shapes=[pltpu.VMEM((k, tn), BF16)],
        compiler_params=_params(("parallel", "arbitrary")),
        name="in_proj",
    )(a, w_all)


POOL_ROWS = 128
assert max(POOL_WINDOWS) - 1 <= POOL_ROWS


def _pool_kernel(z_ref, pw_ref, ps_ref, o_ref, pwb_ref, band_ref):
    s_len = z_ref.shape[0]
    width = len(POOL_WINDOWS) * POOL_CH
    ti = lax.broadcasted_iota(jnp.int32, (POOL_ROWS, 2 * POOL_ROWS), 0)
    kj = lax.broadcasted_iota(jnp.int32, (POOL_ROWS, 2 * POOL_ROWS), 1)
    dist = ti + POOL_ROWS - kj
    for g, w in enumerate(POOL_WINDOWS):
        pwb_ref[g] = pw_ref[g].astype(BF16)
        band_ref[g] = jnp.where((dist >= 0) & (dist < w), 1.0, 0.0).astype(BF16)

    for r in range(0, s_len, POOL_ROWS):
        for g, w in enumerate(POOL_WINDOWS):
            cs = slice(g * POOL_CH, (g + 1) * POOL_CH)
            xb = z_ref[r:r + POOL_ROWS, cs]
            x = xb.astype(F32)
            if r == 0:
                acc = jnp.dot(band_ref[g, :, POOL_ROWS:2 * POOL_ROWS], xb, preferred_element_type=F32)
            else:
                acc = jnp.dot(band_ref[g], z_ref[r - POOL_ROWS:r + POOL_ROWS, cs],
                              preferred_element_type=F32)
            if r + 1 >= w:
                cnt = float(w)
            else:
                row = r + lax.broadcasted_iota(jnp.int32, (POOL_ROWS, POOL_CH), 0)
                cnt = jnp.minimum(row + 1, w).astype(F32)
            pooled = acc / cnt - x
            mixed = jnp.dot(pooled.astype(BF16), pwb_ref[g], preferred_element_type=F32)
            gate = z_ref[r:r + POOL_ROWS, width + g * POOL_CH:width + (g + 1) * POOL_CH].astype(F32)
            y = mixed * ps_ref[:, cs] * _silu(gate)
            o_ref[r:r + POOL_ROWS, cs] = y.astype(o_ref.dtype)


def _pool_mixer(z3, pool_w, pool_scale):
    b, s, _ = z3.shape
    width = len(POOL_WINDOWS) * POOL_CH
    return pl.pallas_call(
        _pool_kernel,
        grid=(b,),
        in_specs=[pl.BlockSpec((None, s, 2 * width), lambda i: (i, 0, 0)),
                  pl.BlockSpec(pool_w.shape, lambda i: (0, 0, 0)),
                  pl.BlockSpec((1, width), lambda i: (0, 0))],
        out_specs=pl.BlockSpec((None, s, width), lambda i: (i, 0, 0)),
        out_shape=jax.ShapeDtypeStruct((b, s, width), BF16),
        scratch_shapes=[pltpu.VMEM(pool_w.shape, BF16),
                        pltpu.VMEM((len(POOL_WINDOWS), POOL_ROWS, 2 * POOL_ROWS), BF16)],
        compiler_params=_params(("parallel",)),
        name="pool_mixer",
    )(z3, pool_w, pool_scale.reshape(1, width))


ATT_SCALE_LOG2 = HEAD_DIM ** -0.5 * float(np.log2(np.e))
ATT_ROWS = 1024
ATT_PREP_ROWS = 2048
ATT_MAX_BATCH = 16


def _for_chunks(n, body):
    if n == 1:
        body(0, 0)
    else:
        lax.fori_loop(0, n, body, 0)


def _aligned(start, align):
    return start if isinstance(start, int) else pl.multiple_of(start, align)


def _attn_kernel(q0, k0, v0, q1, k1, v1, q2, k2, v2, gate_ref, cos_ref, sin_ref,
                 o_ref, qkd_ref, qkv_ref, bias_ref, s_ref, ms_ref, acc_ref, m_ref, l_ref):
    s_len = o_ref.shape[0]
    qkv_in = ((q0, k0, v0), (q1, k1, v1), (q2, k2, v2))

    src = lax.broadcasted_iota(jnp.int32, (HEAD_DIM, HEAD_DIM), 0)
    dst = lax.broadcasted_iota(jnp.int32, (HEAD_DIM, HEAD_DIM), 1)
    perm = (jnp.where((dst < ROT_HALF) & (src == dst + ROT_HALF), -1.0, 0.0)
            + jnp.where((dst >= ROT_HALF) & (dst < ROT_DIM) & (src == dst - ROT_HALF), 1.0, 0.0)
            ).astype(BF16)

    strided = [g for g, d in enumerate(DILATIONS) if d > 1]
    dense = [g for g, d in enumerate(DILATIONS) if d == 1]

    prep_rows, merge_rows = min(ATT_PREP_ROWS, s_len), min(ATT_ROWS, s_len)

    def prep(i, c):
        chunk = pl.ds(_aligned(i * prep_rows, prep_rows), prep_rows)
        cos, sin = cos_ref[chunk, :], sin_ref[chunk, :]
        for g, d in enumerate(DILATIONS):
            for j in range(2):
                t = qkv_in[g][j][chunk, :]
                partner = jnp.dot(t, perm, preferred_element_type=F32)
                roped = t.astype(F32) * cos + partner * sin
                if d == 1:
                    qkd_ref[2 * dense.index(g) + j, chunk, :] = roped.astype(BF16)
                else:
                    qkv_ref[3 * strided.index(g) + j, chunk, :] = roped
            if d > 1:
                qkv_ref[3 * strided.index(g) + 2, chunk, :] = qkv_in[g][2][chunk, :].astype(F32)
        return c

    _for_chunks(s_len // prep_rows, prep)

    qi = lax.broadcasted_iota(jnp.int32, (SPAN, 2 * SPAN), 0)
    kj = lax.broadcasted_iota(jnp.int32, (SPAN, 2 * SPAN), 1)
    band, causal = 0, 1
    bias_ref[band] = jnp.where((kj >= qi) & (kj <= qi + SPAN), 0.0, NEG)
    bias_ref[causal] = jnp.where(kj <= qi, 0.0, NEG)

    def rows(d, start, cnt):
        return pl.ds(_aligned(start, SPAN), cnt) if d == 1 else pl.ds(start, cnt, stride=d)

    def operand(g, d, j, start, cnt):
        if d == 1:
            src = qkv_in[g][2] if j == 2 else qkd_ref.at[2 * dense.index(g) + j]
            return src[rows(d, start, cnt), :]
        return qkv_ref[3 * strided.index(g) + j, rows(d, start, cnt), :].astype(BF16)

    def run_batch(blocks, n_keys, mask):
        assert len(blocks) <= ATT_MAX_BATCH

        def blk(j):
            return slice(j * SPAN, (j + 1) * SPAN)

        for j, (g, d, q_start, k_start) in enumerate(blocks):
            s = lax.dot_general(operand(g, d, 0, q_start, SPAN), operand(g, d, 1, k_start, n_keys),
                                (((1,), (1,)), ((), ())), preferred_element_type=F32)
            s = s * ATT_SCALE_LOG2 + bias_ref[mask, :, 0:n_keys]
            s_ref[blk(j), 0:n_keys] = s
            m = jnp.max(s, axis=-1, keepdims=True)
            ms_ref[blk(j), :] = jnp.broadcast_to(m, (SPAN, HEAD_DIM))

        for j, (g, d, q_start, k_start) in enumerate(blocks):
            m = ms_ref[blk(j), :]
            den, ps = None, []
            for c in range(n_keys // HEAD_DIM):
                p = jnp.exp2(s_ref[blk(j), c * HEAD_DIM:(c + 1) * HEAD_DIM] - m)
                ps.append(p.astype(BF16))
                den = p if den is None else den + p
            den = jnp.sum(den, axis=-1, keepdims=True)
            m_ref[g, rows(d, q_start, SPAN), :] = m
            l_ref[g, rows(d, q_start, SPAN), :] = jnp.broadcast_to(den, (SPAN, HEAD_DIM))
            pb = ps[0] if len(ps) == 1 else jnp.concatenate(ps, axis=1)
            acc_ref[g, rows(d, q_start, SPAN), :] = jnp.dot(
                pb, operand(g, d, 2, k_start, n_keys), preferred_element_type=F32)

    first_blocks = [(g, d, r, r) for g, d in enumerate(DILATIONS) for r in range(d)]
    for lo in range(0, len(first_blocks), ATT_MAX_BATCH):
        run_batch(first_blocks[lo:lo + ATT_MAX_BATCH], SPAN, causal)
    for g, d in enumerate(DILATIONS):
        stride_blk = d * SPAN
        band_blocks = [(g, d, r + i * stride_blk, r + (i - 1) * stride_blk)
                       for i in range(1, s_len // stride_blk) for r in range(d)]
        for lo in range(0, len(band_blocks), ATT_MAX_BATCH):
            run_batch(band_blocks[lo:lo + ATT_MAX_BATCH], 2 * SPAN, band)

    def merge(i, c):
        rows = pl.ds(_aligned(i * merge_rows, merge_rows), merge_rows)
        ms = [m_ref[g, rows, :] for g in range(ATT_GROUPS)]
        m = functools.reduce(jnp.maximum, ms)
        num = den = None
        for g in range(ATT_GROUPS):
            e = jnp.exp2(ms[g] - m)
            ng, dg = e * acc_ref[g, rows, :], e * l_ref[g, rows, :]
            num, den = (ng, dg) if num is None else (num + ng, den + dg)
        gate = gate_ref[rows, :].astype(F32)
        o_ref[rows, :] = (num / den * _silu(gate)).astype(o_ref.dtype)
        return c

    _for_chunks(s_len // merge_rows, merge)


def _rope_tables(s_len):
    pos = jnp.arange(s_len, dtype=F32)
    inv_freq = jnp.power(ROPE_THETA, -jnp.arange(0, ROT_DIM, 2, dtype=F32) / ROT_DIM)
    ang = pos[:, None] * inv_freq[None, :]
    cos, sin = jnp.cos(ang), jnp.sin(ang)
    rest = jnp.zeros((s_len, HEAD_DIM - ROT_DIM), F32)
    return (jnp.concatenate([cos, cos, rest + 1.0], axis=-1),
            jnp.concatenate([sin, sin, rest], axis=-1))


def _dilated_attention(z3, q_col, n_heads):
    b, s, _ = z3.shape
    per = ATT_GROUPS * n_heads
    qb = q_col // HEAD_DIM
    cos_t, sin_t = _rope_tables(s)

    def head_spec(blk0):
        return pl.BlockSpec((None, s, HEAD_DIM), lambda i, h, blk0=blk0: (i, 0, blk0 + h))

    in_specs = []
    for g in range(ATT_GROUPS):
        for j in range(3):
            in_specs.append(head_spec(qb + j * per + g * n_heads))
    in_specs.append(head_spec(qb + 3 * per))
    tab_spec = pl.BlockSpec((s, HEAD_DIM), lambda i, h: (0, 0))
    in_specs += [tab_spec, tab_spec]
    return pl.pallas_call(
        _attn_kernel,
        grid=(b, n_heads),
        in_specs=in_specs,
        out_specs=pl.BlockSpec((None, s, HEAD_DIM), lambda i, h: (i, 0, h)),
        out_shape=jax.ShapeDtypeStruct((b, s, n_heads * HEAD_DIM), BF16),
        scratch_shapes=[pltpu.VMEM((2 * sum(d == 1 for d in DILATIONS), s, HEAD_DIM), BF16),
                        pltpu.VMEM((3 * sum(d > 1 for d in DILATIONS), s, HEAD_DIM), F32),
                        pltpu.VMEM((2, SPAN, 2 * SPAN), F32),
                        pltpu.VMEM((ATT_MAX_BATCH * SPAN, 2 * SPAN), F32),
                        pltpu.VMEM((ATT_MAX_BATCH * SPAN, HEAD_DIM), F32),
                        pltpu.VMEM((ATT_GROUPS, s, HEAD_DIM), F32),
                        pltpu.VMEM((ATT_GROUPS, s, HEAD_DIM), F32),
                        pltpu.VMEM((ATT_GROUPS, s, HEAD_DIM), F32)],
        compiler_params=_params(("parallel", "parallel")),
        name="dilated_attention",
    )(*([z3] * 10), cos_t, sin_t)


OUT_SUB_ROWS = 128


def _out_proj_kernel(*refs, with_next, conv_in):
    refs = list(refs)
    ya_ref = refs.pop(0)
    if conv_in:
        conv_ref, gate_ref, ng_ref, nb_ref = refs[:4]
        del refs[:4]
    else:
        yb_ref = refs.pop(0)
    w_ref, x_ref, gpost_ref = refs[:3]
    del refs[:3]
    if with_next:
        gnext_ref, xo_ref, ho_ref, wb_ref = refs
    else:
        xo_ref, wb_ref = refs

    @pl.when(pl.program_id(0) == 0)
    def _():
        _cast_weight(w_ref, wb_ref)

    half = ya_ref.shape[1]
    for r in range(0, ya_ref.shape[0], OUT_SUB_ROWS):
        rows = slice(r, r + OUT_SUB_ROWS)
        if conv_in:
            c = conv_ref[rows, :]
            mu = jnp.mean(c, axis=-1, keepdims=True)
            cc = c - mu
            var = jnp.mean(cc * cc, axis=-1, keepdims=True)
            yb = _silu(cc * lax.rsqrt(var + EPS) * ng_ref[...] + nb_ref[...])
            yb = (yb * _silu(gate_ref[rows, :].astype(F32))).astype(BF16)
        else:
            yb = yb_ref[rows, :]
        y = jnp.dot(ya_ref[rows, :], wb_ref[0:half, :], preferred_element_type=F32)
        y = y + jnp.dot(yb, wb_ref[half:2 * half, :], preferred_element_type=F32)
        yn = y * lax.rsqrt(jnp.mean(y * y, axis=-1, keepdims=True) + EPS) * gpost_ref[...]
        xn = x_ref[rows, :] + yn
        xo_ref[rows, :] = xn
        if with_next:
            hn = xn * lax.rsqrt(jnp.mean(xn * xn, axis=-1, keepdims=True) + EPS) * gnext_ref[...]
            ho_ref[rows, :] = hn.astype(ho_ref.dtype)


def _out_proj(ya, yb, w_all, layer, x, g_post, g_next=None, tm=512):
    t, half = ya.shape
    _, k, d = w_all.shape
    with_next = g_next is not None
    conv_in = isinstance(yb, tuple)
    row = lambda i: (i, 0)
    fixed = lambda i: (0, 0)
    in_specs = [pl.BlockSpec((tm, half), row)]
    args = [ya]
    if conv_in:
        conv, z, gate_blk, cn_g, cn_b = yb
        in_specs += [pl.BlockSpec((tm, half), row),
                     pl.BlockSpec((tm, half), lambda i: (i, gate_blk)),
                     pl.BlockSpec((1, half), fixed), pl.BlockSpec((1, half), fixed)]
        args += [conv, z, cn_g.reshape(1, half), cn_b.reshape(1, half)]
    else:
        in_specs.append(pl.BlockSpec((tm, half), row))
        args.append(yb)
    in_specs += [pl.BlockSpec((None, k, d), lambda i: (layer, 0, 0), pipeline_mode=pl.Buffered(1)),
                 pl.BlockSpec((tm, d), row),
                 pl.BlockSpec((1, d), fixed)]
    args += [w_all, x, g_post.reshape(1, d)]
    out_specs = [pl.BlockSpec((tm, d), row)]
    out_shape = [jax.ShapeDtypeStruct((t, d), F32)]
    if with_next:
        in_specs.append(pl.BlockSpec((1, d), fixed))
        args.append(g_next.reshape(1, d))
        out_specs.append(pl.BlockSpec((tm, d), row))
        out_shape.append(jax.ShapeDtypeStruct((t, d), BF16))
    return pl.pallas_call(
        functools.partial(_out_proj_kernel, with_next=with_next, conv_in=conv_in),
        grid=(t // tm,),
        in_specs=in_specs,
        out_specs=out_specs,
        out_shape=out_shape,
        scratch_shapes=[pltpu.VMEM((k, d), BF16)],
        compiler_params=_params(("arbitrary",)),
        name="out_proj",
    )(*args)


def _sgu_kernel(u_ref, v_ref, cg_ref, g_ref, b_ref, ws_ref, bias_ref, o_ref):
    tm = u_ref.shape[0]
    ii = lax.broadcasted_iota(jnp.int32, (CHUNK, CHUNK), 0)
    jj = lax.broadcasted_iota(jnp.int32, (CHUNK, CHUNK), 1)
    tril = jj <= ii
    for c in range(tm // CHUNK):
        rows = slice(c * CHUNK, (c + 1) * CHUNK)
        v = v_ref[rows, :].astype(F32)
        mu = jnp.mean(v, axis=-1, keepdims=True)
        vc = v - mu
        var = jnp.mean(vc * vc, axis=-1, keepdims=True)
        vn = (vc * lax.rsqrt(var + EPS) * g_ref[...] + b_ref[...]).astype(BF16)
        for h in range(SGU_GROUPS):
            cs = slice(h * SGU_CH, (h + 1) * SGU_CH)
            wm = jnp.where(tril, ws_ref[h], 0.0).astype(BF16)
            sg = jnp.dot(wm, vn[:, cs], preferred_element_type=F32) + bias_ref[:, cs]
            y = u_ref[rows, cs].astype(F32) * sg * _silu(cg_ref[rows, cs].astype(F32))
            o_ref[rows, cs] = y.astype(o_ref.dtype)


def _sgu(z, sgu_g, sgu_b, sgu_w, sgu_bias, tm=1024):
    t = z.shape[0]
    width = SGU_GROUPS * SGU_CH
    bias_full = jnp.repeat(sgu_bias.T, SGU_CH, axis=1)
    fixed2 = lambda i: (0, 0)
    return pl.pallas_call(
        _sgu_kernel,
        grid=(t // tm,),
        in_specs=[pl.BlockSpec((tm, width), lambda i: (i, 0)),
                  pl.BlockSpec((tm, width), lambda i: (i, 1)),
                  pl.BlockSpec((tm, width), lambda i: (i, 2)),
                  pl.BlockSpec((1, width), fixed2), pl.BlockSpec((1, width), fixed2),
                  pl.BlockSpec(sgu_w.shape, lambda i: (0, 0, 0)),
                  pl.BlockSpec((CHUNK, width), fixed2)],
        out_specs=pl.BlockSpec((tm, width), lambda i: (i, 0)),
        out_shape=jax.ShapeDtypeStruct((t, width), BF16),
        compiler_params=_params(("parallel",)),
        name="sgu",
    )(z, z, z, sgu_g.reshape(1, width), sgu_b.reshape(1, width), sgu_w, bias_full)


CONV_PAD = 32
CONV_ROWS = 128
CONV_FILL = 128
CONV_SLABS = 2


def _conv_kernel(dv_ref, dg_ref, w_ref, cb_ref, o_ref, dpad_ref):
    s_len = dv_ref.shape[0]
    for c in range(CONV_SLABS):
        dpad_ref[c, 0:CONV_PAD, :] = jnp.zeros((CONV_PAD, LANES), F32)

    def fill(i, carry):
        r = pl.multiple_of(i * CONV_FILL, CONV_FILL)
        rows = pl.ds(r, CONV_FILL)
        d = dv_ref[rows, :].astype(F32) * jax.nn.sigmoid(dg_ref[rows, :].astype(F32))
        for c in range(CONV_SLABS):
            dpad_ref[c, pl.ds(CONV_PAD + r, CONV_FILL), :] = d[:, c * LANES:(c + 1) * LANES]
        return carry

    lax.fori_loop(0, s_len // CONV_FILL, fill, 0)

    lead = CONV_PAD - (CONV_K - 1)
    for c in range(CONV_SLABS):
        cs = slice(c * LANES, (c + 1) * LANES)
        taps = [w_ref[k:k + 1, cs] for k in range(CONV_K)]
        for r in range(0, s_len, CONV_ROWS):
            acc = dpad_ref[c, r + lead:r + lead + CONV_ROWS, :] * taps[0]
            for k in range(1, CONV_K):
                acc = acc + dpad_ref[c, r + lead + k:r + lead + k + CONV_ROWS, :] * taps[k]
            o_ref[r:r + CONV_ROWS, cs] = acc + cb_ref[:, cs]


def _conv_taps(z3, col, conv_w, conv_b):
    b, s, cols = z3.shape
    ch = conv_w.shape[1]
    width = CONV_SLABS * LANES
    n_steps = ch // width
    blk = col // width
    vec = lambda a: a.reshape(1, ch)
    return pl.pallas_call(
        _conv_kernel,
        grid=(b, n_steps),
        in_specs=[pl.BlockSpec((None, s, width), lambda i, c: (i, 0, blk + c)),
                  pl.BlockSpec((None, s, width), lambda i, c: (i, 0, blk + n_steps + c)),
                  pl.BlockSpec((CONV_K, width), lambda i, c: (0, c)),
                  pl.BlockSpec((1, width), lambda i, c: (0, c))],
        out_specs=pl.BlockSpec((None, s, width), lambda i, c: (i, 0, c)),
        out_shape=jax.ShapeDtypeStruct((b, s, ch), F32),
        scratch_shapes=[pltpu.VMEM((CONV_SLABS, CONV_PAD + s, LANES), F32)],
        compiler_params=_params(("parallel", "parallel")),
        name="conv_taps",
    )(z3, z3, conv_w, vec(conv_b))


def kernel(x, e_pre_norm, e_w_in, e_pool_w, e_pool_scale, e_w_out, e_post_norm, o_pre_norm, o_w_in, o_sgu_norm_g, o_sgu_norm_b, o_sgu_w, o_sgu_b, o_conv_w, o_conv_b, o_conv_norm_g, o_conv_norm_b, o_w_out, o_post_norm):
    b, s, d = x.shape
    t = b * s
    n_even, n_odd = e_w_in.shape[0], o_w_in.shape[0]
    depth = n_even + n_odd
    pool_width = len(POOL_WINDOWS) * POOL_CH
    n_heads = (e_w_out.shape[1] - pool_width) // HEAD_DIM
    sgu_width = SGU_GROUPS * SGU_CH

    def pre_gain(i):
        return e_pre_norm[i // 2] if i % 2 == 0 else o_pre_norm[i // 2]

    xf = x.reshape(t, d)
    h = _rmsnorm(xf, pre_gain(0))
    for i in range(depth):
        j = i // 2
        g_next = pre_gain(i + 1) if i + 1 < depth else None
        if i % 2 == 0:
            z = _in_proj(h, e_w_in, j)
            z3 = z.reshape(b, s, z.shape[1])
            ya = _pool_mixer(z3, e_pool_w[j], e_pool_scale[j])
            yb = _dilated_attention(z3, 2 * pool_width, n_heads)
            w_out, g_post = e_w_out, e_post_norm[j]
        else:
            z = _in_proj(h, o_w_in, j)
            z3 = z.reshape(b, s, z.shape[1])
            ya = _sgu(z, o_sgu_norm_g[j], o_sgu_norm_b[j], o_sgu_w[j], o_sgu_b[j])
            conv_width = o_conv_w.shape[2]
            conv = _conv_taps(z3, 3 * sgu_width, o_conv_w[j], o_conv_b[j])
            gate_blk = (3 * sgu_width + 2 * conv_width) // conv_width
            yb = (conv.reshape(t, conv_width), z, gate_blk, o_conv_norm_g[j], o_conv_norm_b[j])
            w_out, g_post = o_w_out, o_post_norm[j]
        if not isinstance(yb, tuple):
            yb = yb.reshape(t, -1)
        outs = _out_proj(ya.reshape(t, -1), yb, w_out, j, xf, g_post, g_next)
        if g_next is None:
            xf = outs[0]
        else:
            xf, h = outs
    return xf.reshape(b, s, d)
```

```python
import functools
import itertools

import jax
import jax.numpy as jnp
import numpy as np
from jax import lax
from jax.experimental import pallas as pl
from jax.experimental.pallas import tpu as pltpu

F32 = jnp.float32
BF16 = jnp.bfloat16

EPS = 1e-6
NEG = -1e30
HEAD_DIM = 128
ROT_DIM = HEAD_DIM // 4
ROT_HALF = ROT_DIM // 2
ROPE_THETA = 500000.0
DILATIONS = (1, 4, 16)
SPAN = 128
ATT_GROUPS = len(DILATIONS)
POOL_WINDOWS = (2, 4, 8, 16)
POOL_CH = 256
SGU_GROUPS = 4
SGU_CH = 256
CHUNK = 128
CONV_K = 31

LANES = 128
VMEM_LIMIT = 56 * 1024 * 1024


def _params(sem, vmem=VMEM_LIMIT):
    return pltpu.CompilerParams(dimension_semantics=sem, vmem_limit_bytes=vmem)


def _silu(x):
    return x * jax.nn.sigmoid(x)


def _rmsnorm_kernel(x_ref, g_ref, o_ref):
    x = x_ref[...]
    ms = jnp.mean(x * x, axis=-1, keepdims=True)
    o_ref[...] = (x * lax.rsqrt(ms + EPS) * g_ref[...]).astype(o_ref.dtype)


def _rmsnorm(x, g, tm=1024):
    t, d = x.shape
    return pl.pallas_call(
        _rmsnorm_kernel,
        grid=(t // tm,),
        in_specs=[pl.BlockSpec((tm, d), lambda i: (i, 0)),
                  pl.BlockSpec((1, d), lambda i: (0, 0))],
        out_specs=pl.BlockSpec((tm, d), lambda i: (i, 0)),
        out_shape=jax.ShapeDtypeStruct((t, d), BF16),
        compiler_params=_params(("parallel",)),
        name="rmsnorm",
    )(x, g.reshape(1, d))


CAST_ROWS = 256


def _cast_weight(w_ref, wb_ref):
    def chunk(i, carry):
        rows = pl.ds(pl.multiple_of(i * CAST_ROWS, CAST_ROWS), CAST_ROWS)
        wb_ref[rows, :] = w_ref[rows, :].astype(wb_ref.dtype)
        return carry

    lax.fori_loop(0, w_ref.shape[0] // CAST_ROWS, chunk, 0)


def _in_proj_kernel(a_ref, w_ref, o_ref, wb_ref):
    first = pl.program_id(1) == 0

    @pl.when(first)
    def _():
        acc = None
        for c in range(0, w_ref.shape[0], CAST_ROWS):
            wc = w_ref[c:c + CAST_ROWS, :].astype(BF16)
            wb_ref[c:c + CAST_ROWS, :] = wc
            part = jnp.dot(a_ref[:, c:c + CAST_ROWS], wc, preferred_element_type=F32)
            acc = part if acc is None else acc + part
        o_ref[...] = acc.astype(o_ref.dtype)

    @pl.when(jnp.logical_not(first))
    def _():
        o_ref[...] = jnp.dot(a_ref[...], wb_ref[...],
                             preferred_element_type=F32).astype(o_ref.dtype)


def _in_proj(a, w_all, layer, tm=2048, tn=1024):
    m, k = a.shape
    n = w_all.shape[2]
    return pl.pallas_call(
        _in_proj_kernel,
        grid=(n // tn, m // tm),
        in_specs=[pl.BlockSpec((tm, k), lambda j, i: (i, 0)),
                  pl.BlockSpec((None, k, tn), lambda j, i: (layer, 0, j))],
        out_specs=pl.BlockSpec((tm, tn), lambda j, i: (i, j)),
        out_shape=jax.ShapeDtypeStruct((m, n), BF16),
        scratch_shapes=[pltpu.VMEM((k, tn), BF16)],
        compiler_params=_params(("parallel", "arbitrary")),
        name="in_proj",
    )(a, w_all)


POOL_ROWS = 128
assert max(POOL_WINDOWS) - 1 <= POOL_ROWS


def _pool_kernel(z_ref, pw_ref, ps_ref, o_ref, pwb_ref, band_ref):
    s_len = z_ref.shape[0]
    width = len(POOL_WINDOWS) * POOL_CH
    ti = lax.broadcasted_iota(jnp.int32, (POOL_ROWS, 2 * POOL_ROWS), 0)
    kj = lax.broadcasted_iota(jnp.int32, (POOL_ROWS, 2 * POOL_ROWS), 1)
    dist = ti + POOL_ROWS - kj
    for g, w in enumerate(POOL_WINDOWS):
        pwb_ref[g] = pw_ref[g].astype(BF16)
        band_ref[g] = jnp.where((dist >= 0) & (dist < w), 1.0, 0.0).astype(BF16)

    for r in range(0, s_len, POOL_ROWS):
        for g, w in enumerate(POOL_WINDOWS):
            cs = slice(g * POOL_CH, (g + 1) * POOL_CH)
            xb = z_ref[r:r + POOL_ROWS, cs]
            x = xb.astype(F32)
            if r == 0:
                acc = jnp.dot(band_ref[g, :, POOL_ROWS:2 * POOL_ROWS], xb, preferred_element_type=F32)
            else:
                acc = jnp.dot(band_ref[g], z_ref[r - POOL_ROWS:r + POOL_ROWS, cs],
                              preferred_element_type=F32)
            if r + 1 >= w:
                cnt = float(w)
            else:
                row = r + lax.broadcasted_iota(jnp.int32, (POOL_ROWS, POOL_CH), 0)
                cnt = jnp.minimum(row + 1, w).astype(F32)
            pooled = acc / cnt - x
            mixed = jnp.dot(pooled.astype(BF16), pwb_ref[g], preferred_element_type=F32)
            gate = z_ref[r:r + POOL_ROWS, width + g * POOL_CH:width + (g + 1) * POOL_CH].astype(F32)
            y = mixed * ps_ref[:, cs] * _silu(gate)
            o_ref[r:r + POOL_ROWS, cs] = y.astype(o_ref.dtype)


def _pool_mixer(z3, pool_w, pool_scale):
    b, s, _ = z3.shape
    width = len(POOL_WINDOWS) * POOL_CH
    return pl.pallas_call(
        _pool_kernel,
        grid=(b,),
        in_specs=[pl.BlockSpec((None, s, 2 * width), lambda i: (i, 0, 0)),
                  pl.BlockSpec(pool_w.shape, lambda i: (0, 0, 0)),
                  pl.BlockSpec((1, width), lambda i: (0, 0))],
        out_specs=pl.BlockSpec((None, s, width), lambda i: (i, 0, 0)),
        out_shape=jax.ShapeDtypeStruct((b, s, width), BF16),
        scratch_shapes=[pltpu.VMEM(pool_w.shape, BF16),
                        pltpu.VMEM((len(POOL_WINDOWS), POOL_ROWS, 2 * POOL_ROWS), BF16)],
        compiler_params=_params(("parallel",)),
        name="pool_mixer",
    )(z3, pool_w, pool_scale.reshape(1, width))


ATT_SCALE_LOG2 = HEAD_DIM ** -0.5 * float(np.log2(np.e))
ATT_ROWS = 1024
ATT_PREP_ROWS = 2048
ATT_MAX_BATCH = 12


def _for_chunks(n, body):
    if n == 1:
        body(0, 0)
    else:
        lax.fori_loop(0, n, body, 0)


def _aligned(start, align):
    return start if isinstance(start, int) else pl.multiple_of(start, align)


def _attn_kernel(q0, k0, v0, q1, k1, v1, q2, k2, v2, gate_ref, cos_ref, sin_ref,
                 o_ref, qkd_ref, qkv_ref, bias_ref, s_ref, ms_ref, acc_ref, m_ref, l_ref):
    s_len = o_ref.shape[0]
    qkv_in = ((q0, k0, v0), (q1, k1, v1), (q2, k2, v2))

    src = lax.broadcasted_iota(jnp.int32, (HEAD_DIM, HEAD_DIM), 0)
    dst = lax.broadcasted_iota(jnp.int32, (HEAD_DIM, HEAD_DIM), 1)
    perm = (jnp.where((dst < ROT_HALF) & (src == dst + ROT_HALF), -1.0, 0.0)
            + jnp.where((dst >= ROT_HALF) & (dst < ROT_DIM) & (src == dst - ROT_HALF), 1.0, 0.0)
            ).astype(BF16)

    strided = [g for g, d in enumerate(DILATIONS) if d > 1]
    dense = [g for g, d in enumerate(DILATIONS) if d == 1]

    prep_rows, merge_rows = min(ATT_PREP_ROWS, s_len), min(ATT_ROWS, s_len)

    def prep(i, c):
        chunk = pl.ds(_aligned(i * prep_rows, prep_rows), prep_rows)
        cos, sin = cos_ref[chunk, :], sin_ref[chunk, :]
        for g, d in enumerate(DILATIONS):
            for j in range(2):
                t = qkv_in[g][j][chunk, :]
                partner = jnp.dot(t, perm, preferred_element_type=F32)
                roped = t.astype(F32) * cos + partner * sin
                if d == 1:
                    qkd_ref[2 * dense.index(g) + j, chunk, :] = roped.astype(BF16)
                else:
                    qkv_ref[3 * strided.index(g) + j, chunk, :] = roped
            if d > 1:
                qkv_ref[3 * strided.index(g) + 2, chunk, :] = qkv_in[g][2][chunk, :].astype(F32)
        return c

    _for_chunks(s_len // prep_rows, prep)

    qi = lax.broadcasted_iota(jnp.int32, (SPAN, 2 * SPAN), 0)
    kj = lax.broadcasted_iota(jnp.int32, (SPAN, 2 * SPAN), 1)
    band, causal = 0, 1
    bias_ref[band] = jnp.where((kj >= qi) & (kj <= qi + SPAN), 0.0, NEG)
    bias_ref[causal] = jnp.where(kj <= qi, 0.0, NEG)

    def rows(d, start, cnt):
        return pl.ds(_aligned(start, SPAN), cnt) if d == 1 else pl.ds(start, cnt, stride=d)

    def operand(g, d, j, start, cnt):
        if d == 1:
            src = qkv_in[g][2] if j == 2 else qkd_ref.at[2 * dense.index(g) + j]
            return src[rows(d, start, cnt), :]
        return qkv_ref[3 * strided.index(g) + j, rows(d, start, cnt), :].astype(BF16)

    def run_batch(blocks):
        assert len(blocks) <= ATT_MAX_BATCH

        def blk(j):
            return slice(j * SPAN, (j + 1) * SPAN)

        for j, (g, d, q_start, k_start, n_keys, mask) in enumerate(blocks):
            s = lax.dot_general(operand(g, d, 0, q_start, SPAN), operand(g, d, 1, k_start, n_keys),
                                (((1,), (1,)), ((), ())), preferred_element_type=F32)
            s = s * ATT_SCALE_LOG2 + bias_ref[mask, :, 0:n_keys]
            s_ref[blk(j), 0:n_keys] = s
            m = jnp.max(s, axis=-1, keepdims=True)
            ms_ref[blk(j), :] = jnp.broadcast_to(m, (SPAN, HEAD_DIM))

        for j, (g, d, q_start, k_start, n_keys, mask) in enumerate(blocks):
            m = ms_ref[blk(j), :]
            den, ps = None, []
            for c in range(n_keys // HEAD_DIM):
                p = jnp.exp2(s_ref[blk(j), c * HEAD_DIM:(c + 1) * HEAD_DIM] - m)
                ps.append(p.astype(BF16))
                den = p if den is None else den + p
            den = jnp.sum(den, axis=-1, keepdims=True)
            m_ref[g, rows(d, q_start, SPAN), :] = m
            l_ref[g, rows(d, q_start, SPAN), :] = jnp.broadcast_to(den, (SPAN, HEAD_DIM))
            pb = ps[0] if len(ps) == 1 else jnp.concatenate(ps, axis=1)
            acc_ref[g, rows(d, q_start, SPAN), :] = jnp.dot(
                pb, operand(g, d, 2, k_start, n_keys), preferred_element_type=F32)

    per_group = []
    for g, d in enumerate(DILATIONS):
        stride_blk = d * SPAN
        blocks = [(g, d, r, r, SPAN, causal) for r in range(d)]
        blocks += [(g, d, r + i * stride_blk, r + (i - 1) * stride_blk, 2 * SPAN, band)
                   for i in range(1, s_len // stride_blk) for r in range(d)]
        per_group.append(blocks)
    mixed = [b for tup in itertools.zip_longest(*per_group) for b in tup if b is not None]
    for lo in range(0, len(mixed), ATT_MAX_BATCH):
        run_batch(mixed[lo:lo + ATT_MAX_BATCH])

    def merge(i, c):
        rows = pl.ds(_aligned(i * merge_rows, merge_rows), merge_rows)
        ms = [m_ref[g, rows, :] for g in range(ATT_GROUPS)]
        m = functools.reduce(jnp.maximum, ms)
        num = den = None
        for g in range(ATT_GROUPS):
            e = jnp.exp2(ms[g] - m)
            ng, dg = e * acc_ref[g, rows, :], e * l_ref[g, rows, :]
            num, den = (ng, dg) if num is None else (num + ng, den + dg)
        gate = gate_ref[rows, :].astype(F32)
        o_ref[rows, :] = (num / den * _silu(gate)).astype(o_ref.dtype)
        return c

    _for_chunks(s_len // merge_rows, merge)


def _rope_tables(s_len):
    pos = jnp.arange(s_len, dtype=F32)
    inv_freq = jnp.power(ROPE_THETA, -jnp.arange(0, ROT_DIM, 2, dtype=F32) / ROT_DIM)
    ang = pos[:, None] * inv_freq[None, :]
    cos, sin = jnp.cos(ang), jnp.sin(ang)
    rest = jnp.zeros((s_len, HEAD_DIM - ROT_DIM), F32)
    return (jnp.concatenate([cos, cos, rest + 1.0], axis=-1),
            jnp.concatenate([sin, sin, rest], axis=-1))


def _dilated_attention(z3, q_col, n_heads):
    b, s, _ = z3.shape
    per = ATT_GROUPS * n_heads
    qb = q_col // HEAD_DIM
    cos_t, sin_t = _rope_tables(s)

    def head_spec(blk0):
        return pl.BlockSpec((None, s, HEAD_DIM), lambda i, h, blk0=blk0: (i, 0, blk0 + h))

    in_specs = []
    for g in range(ATT_GROUPS):
        for j in range(3):
            in_specs.append(head_spec(qb + j * per + g * n_heads))
    in_specs.append(head_spec(qb + 3 * per))
    tab_spec = pl.BlockSpec((s, HEAD_DIM), lambda i, h: (0, 0))
    in_specs += [tab_spec, tab_spec]
    return pl.pallas_call(
        _attn_kernel,
        grid=(b, n_heads),
        in_specs=in_specs,
        out_specs=pl.BlockSpec((None, s, HEAD_DIM), lambda i, h: (i, 0, h)),
        out_shape=jax.ShapeDtypeStruct((b, s, n_heads * HEAD_DIM), BF16),
        scratch_shapes=[pltpu.VMEM((2 * sum(d == 1 for d in DILATIONS), s, HEAD_DIM), BF16),
                        pltpu.VMEM((3 * sum(d > 1 for d in DILATIONS), s, HEAD_DIM), F32),
                        pltpu.VMEM((2, SPAN, 2 * SPAN), F32),
                        pltpu.VMEM((ATT_MAX_BATCH * SPAN, 2 * SPAN), F32),
                        pltpu.VMEM((ATT_MAX_BATCH * SPAN, HEAD_DIM), F32),
                        pltpu.VMEM((ATT_GROUPS, s, HEAD_DIM), F32),
                        pltpu.VMEM((ATT_GROUPS, s, HEAD_DIM), F32),
                        pltpu.VMEM((ATT_GROUPS, s, HEAD_DIM), F32)],
        compiler_params=_params(("parallel", "parallel")),
        name="dilated_attention",
    )(*([z3] * 10), cos_t, sin_t)


OUT_SUB_ROWS = 128


def _out_proj_kernel(*refs, with_next, conv_in):
    refs = list(refs)
    ya_ref = refs.pop(0)
    if conv_in:
        conv_ref, gate_ref, ng_ref, nb_ref = refs[:4]
        del refs[:4]
    else:
        yb_ref = refs.pop(0)
    w_ref, x_ref, gpost_ref = refs[:3]
    del refs[:3]
    if with_next:
        gnext_ref, xo_ref, ho_ref, wb_ref = refs
    else:
        xo_ref, wb_ref = refs

    @pl.when(pl.program_id(0) == 0)
    def _():
        _cast_weight(w_ref, wb_ref)

    half = ya_ref.shape[1]
    for r in range(0, ya_ref.shape[0], OUT_SUB_ROWS):
        rows = slice(r, r + OUT_SUB_ROWS)
        if conv_in:
            c = conv_ref[rows, :]
            mu = jnp.mean(c, axis=-1, keepdims=True)
            cc = c - mu
            var = jnp.mean(cc * cc, axis=-1, keepdims=True)
            yb = _silu(cc * lax.rsqrt(var + EPS) * ng_ref[...] + nb_ref[...])
            yb = (yb * _silu(gate_ref[rows, :].astype(F32))).astype(BF16)
        else:
            yb = yb_ref[rows, :]
        y = jnp.dot(ya_ref[rows, :], wb_ref[0:half, :], preferred_element_type=F32)
        y = y + jnp.dot(yb, wb_ref[half:2 * half, :], preferred_element_type=F32)
        yn = y * lax.rsqrt(jnp.mean(y * y, axis=-1, keepdims=True) + EPS) * gpost_ref[...]
        xn = x_ref[rows, :] + yn
        xo_ref[rows, :] = xn
        if with_next:
            hn = xn * lax.rsqrt(jnp.mean(xn * xn, axis=-1, keepdims=True) + EPS) * gnext_ref[...]
            ho_ref[rows, :] = hn.astype(ho_ref.dtype)


def _out_proj(ya, yb, w_all, layer, x, g_post, g_next=None, tm=512):
    t, half = ya.shape
    _, k, d = w_all.shape
    with_next = g_next is not None
    conv_in = isinstance(yb, tuple)
    row = lambda i: (i, 0)
    fixed = lambda i: (0, 0)
    in_specs = [pl.BlockSpec((tm, half), row)]
    args = [ya]
    if conv_in:
        conv, z, gate_blk, cn_g, cn_b = yb
        in_specs += [pl.BlockSpec((tm, half), row),
                     pl.BlockSpec((tm, half), lambda i: (i, gate_blk)),
                     pl.BlockSpec((1, half), fixed), pl.BlockSpec((1, half), fixed)]
        args += [conv, z, cn_g.reshape(1, half), cn_b.reshape(1, half)]
    else:
        in_specs.append(pl.BlockSpec((tm, half), row))
        args.append(yb)
    in_specs += [pl.BlockSpec((None, k, d), lambda i: (layer, 0, 0), pipeline_mode=pl.Buffered(1)),
                 pl.BlockSpec((tm, d), row),
                 pl.BlockSpec((1, d), fixed)]
    args += [w_all, x, g_post.reshape(1, d)]
    out_specs = [pl.BlockSpec((tm, d), row)]
    out_shape = [jax.ShapeDtypeStruct((t, d), F32)]
    if with_next:
        in_specs.append(pl.BlockSpec((1, d), fixed))
        args.append(g_next.reshape(1, d))
        out_specs.append(pl.BlockSpec((tm, d), row))
        out_shape.append(jax.ShapeDtypeStruct((t, d), BF16))
    return pl.pallas_call(
        functools.partial(_out_proj_kernel, with_next=with_next, conv_in=conv_in),
        grid=(t // tm,),
        in_specs=in_specs,
        out_specs=out_specs,
        out_shape=out_shape,
        scratch_shapes=[pltpu.VMEM((k, d), BF16)],
        compiler_params=_params(("arbitrary",)),
        name="out_proj",
    )(*args)


def _sgu_kernel(u_ref, v_ref, cg_ref, g_ref, b_ref, ws_ref, bias_ref, o_ref):
    tm = u_ref.shape[0]
    ii = lax.broadcasted_iota(jnp.int32, (CHUNK, CHUNK), 0)
    jj = lax.broadcasted_iota(jnp.int32, (CHUNK, CHUNK), 1)
    tril = jj <= ii
    for c in range(tm // CHUNK):
        rows = slice(c * CHUNK, (c + 1) * CHUNK)
        v = v_ref[rows, :].astype(F32)
        mu = jnp.mean(v, axis=-1, keepdims=True)
        vc = v - mu
        var = jnp.mean(vc * vc, axis=-1, keepdims=True)
        vn = (vc * lax.rsqrt(var + EPS) * g_ref[...] + b_ref[...]).astype(BF16)
        for h in range(SGU_GROUPS):
            cs = slice(h * SGU_CH, (h + 1) * SGU_CH)
            wm = jnp.where(tril, ws_ref[h], 0.0).astype(BF16)
            sg = jnp.dot(wm, vn[:, cs], preferred_element_type=F32) + bias_ref[:, cs]
            y = u_ref[rows, cs].astype(F32) * sg * _silu(cg_ref[rows, cs].astype(F32))
            o_ref[rows, cs] = y.astype(o_ref.dtype)


def _sgu(z, sgu_g, sgu_b, sgu_w, sgu_bias, tm=1024):
    t = z.shape[0]
    width = SGU_GROUPS * SGU_CH
    bias_full = jnp.repeat(sgu_bias.T, SGU_CH, axis=1)
    fixed2 = lambda i: (0, 0)
    return pl.pallas_call(
        _sgu_kernel,
        grid=(t // tm,),
        in_specs=[pl.BlockSpec((tm, width), lambda i: (i, 0)),
                  pl.BlockSpec((tm, width), lambda i: (i, 1)),
                  pl.BlockSpec((tm, width), lambda i: (i, 2)),
                  pl.BlockSpec((1, width), fixed2), pl.BlockSpec((1, width), fixed2),
                  pl.BlockSpec(sgu_w.shape, lambda i: (0, 0, 0)),
                  pl.BlockSpec((CHUNK, width), fixed2)],
        out_specs=pl.BlockSpec((tm, width), lambda i: (i, 0)),
        out_shape=jax.ShapeDtypeStruct((t, width), BF16),
        compiler_params=_params(("parallel",)),
        name="sgu",
    )(z, z, z, sgu_g.reshape(1, width), sgu_b.reshape(1, width), sgu_w, bias_full)


CONV_PAD = 32
CONV_ROWS = 128
CONV_FILL = 128
CONV_SLABS = 2


def _conv_kernel(dv_ref, dg_ref, w_ref, cb_ref, o_ref, dpad_ref):
    s_len = dv_ref.shape[0]
    for c in range(CONV_SLABS):
        dpad_ref[c, 0:CONV_PAD, :] = jnp.zeros((CONV_PAD, LANES), F32)

    def fill(i, carry):
        r = pl.multiple_of(i * CONV_FILL, CONV_FILL)
        rows = pl.ds(r, CONV_FILL)
        d = dv_ref[rows, :].astype(F32) * jax.nn.sigmoid(dg_ref[rows, :].astype(F32))
        for c in range(CONV_SLABS):
            dpad_ref[c, pl.ds(CONV_PAD + r, CONV_FILL), :] = d[:, c * LANES:(c + 1) * LANES]
        return carry

    lax.fori_loop(0, s_len // CONV_FILL, fill, 0)

    lead = CONV_PAD - (CONV_K - 1)
    for c in range(CONV_SLABS):
        cs = slice(c * LANES, (c + 1) * LANES)
        taps = [w_ref[k:k + 1, cs] for k in range(CONV_K)]
        for r in range(0, s_len, CONV_ROWS):
            acc = dpad_ref[c, r + lead:r + lead + CONV_ROWS, :] * taps[0]
            for k in range(1, CONV_K):
                acc = acc + dpad_ref[c, r + lead + k:r + lead + k + CONV_ROWS, :] * taps[k]
            o_ref[r:r + CONV_ROWS, cs] = acc + cb_ref[:, cs]


def _conv_taps(z3, col, conv_w, conv_b):
    b, s, cols = z3.shape
    ch = conv_w.shape[1]
    width = CONV_SLABS * LANES
    n_steps = ch // width
    blk = col // width
    vec = lambda a: a.reshape(1, ch)
    return pl.pallas_call(
        _conv_kernel,
        grid=(b, n_steps),
        in_specs=[pl.BlockSpec((None, s, width), lambda i, c: (i, 0, blk + c)),
                  pl.BlockSpec((None, s, width), lambda i, c: (i, 0, blk + n_steps + c)),
                  pl.BlockSpec((CONV_K, width), lambda i, c: (0, c)),
                  pl.BlockSpec((1, width), lambda i, c: (0, c))],
        out_specs=pl.BlockSpec((None, s, width), lambda i, c: (i, 0, c)),
        out_shape=jax.ShapeDtypeStruct((b, s, ch), F32),
        scratch_shapes=[pltpu.VMEM((CONV_SLABS, CONV_PAD + s, LANES), F32)],
        compiler_params=_params(("parallel", "parallel")),
        name="conv_taps",
    )(z3, z3, conv_w, vec(conv_b))


def kernel(x, e_pre_norm, e_w_in, e_pool_w, e_pool_scale, e_w_out, e_post_norm, o_pre_norm, o_w_in, o_sgu_norm_g, o_sgu_norm_b, o_sgu_w, o_sgu_b, o_conv_w, o_conv_b, o_conv_norm_g, o_conv_norm_b, o_w_out, o_post_norm):
    b, s, d = x.shape
    t = b * s
    n_even, n_odd = e_w_in.shape[0], o_w_in.shape[0]
    depth = n_even + n_odd
    pool_width = len(POOL_WINDOWS) * POOL_CH
    n_heads = (e_w_out.shape[1] - pool_width) // HEAD_DIM
    sgu_width = SGU_GROUPS * SGU_CH

    def pre_gain(i):
        return e_pre_norm[i // 2] if i % 2 == 0 else o_pre_norm[i // 2]

    xf = x.reshape(t, d)
    h = _rmsnorm(xf, pre_gain(0))
    for i in range(depth):
        j = i // 2
        g_next = pre_gain(i + 1) if i + 1 < depth else None
        if i % 2 == 0:
            z = _in_proj(h, e_w_in, j)
            z3 = z.reshape(b, s, z.shape[1])
            ya = _pool_mixer(z3, e_pool_w[j], e_pool_scale[j])
            yb = _dilated_attention(z3, 2 * pool_width, n_heads)
            w_out, g_post = e_w_out, e_post_norm[j]
        else:
            z = _in_proj(h, o_w_in, j)
            z3 = z.reshape(b, s, z.shape[1])
            ya = _sgu(z, o_sgu_norm_g[j], o_sgu_norm_b[j], o_sgu_w[j], o_sgu_b[j])
            conv_width = o_conv_w.shape[2]
            conv = _conv_taps(z3, 3 * sgu_width, o_conv_w[j], o_conv_b[j])
            gate_blk = (3 * sgu_width + 2 * conv_width) // conv_width
            yb = (conv.reshape(t, conv_width), z, gate_blk, o_conv_norm_g[j], o_conv_norm_b[j])
            w_out, g_post = o_w_out, o_post_norm[j]
        if not isinstance(yb, tuple):
            yb = yb.reshape(t, -1)
        outs = _out_proj(ya.reshape(t, -1), yb, w_out, j, xf, g_post, g_next)
        if g_next is None:
            xf = outs[0]
        else:
            xf, h = outs
    return xf.reshape(b, s, d)
```

```python
import functools

import jax
import jax.numpy as jnp
import numpy as np
from jax import lax
from jax.experimental import pallas as pl
from jax.experimental.pallas import tpu as pltpu

F32 = jnp.float32
BF16 = jnp.bfloat16

EPS = 1e-6
NEG = -1e30
HEAD_DIM = 128
ROT_DIM = HEAD_DIM // 4
ROT_HALF = ROT_DIM // 2
ROPE_THETA = 500000.0
DILATIONS = (1, 4, 16)
SPAN = 128
ATT_GROUPS = len(DILATIONS)
POOL_WINDOWS = (2, 4, 8, 16)
POOL_CH = 256
SGU_GROUPS = 4
SGU_CH = 256
CHUNK = 128
CONV_K = 31

LANES = 128
VMEM_LIMIT = 56 * 1024 * 1024


def _params(sem, vmem=VMEM_LIMIT):
    return pltpu.CompilerParams(dimension_semantics=sem, vmem_limit_bytes=vmem)


def _silu(x):
    return x * jax.nn.sigmoid(x)


def _rmsnorm_kernel(x_ref, g_ref, o_ref):
    x = x_ref[...]
    ms = jnp.mean(x * x, axis=-1, keepdims=True)
    o_ref[...] = (x * lax.rsqrt(ms + EPS) * g_ref[...]).astype(o_ref.dtype)


def _rmsnorm(x, g, tm=1024):
    t, d = x.shape
    return pl.pallas_call(
        _rmsnorm_kernel,
        grid=(t // tm,),
        in_specs=[pl.BlockSpec((tm, d), lambda i: (i, 0)),
                  pl.BlockSpec((1, d), lambda i: (0, 0))],
        out_specs=pl.BlockSpec((tm, d), lambda i: (i, 0)),
        out_shape=jax.ShapeDtypeStruct((t, d), BF16),
        compiler_params=_params(("parallel",)),
        name="rmsnorm",
    )(x, g.reshape(1, d))


CAST_ROWS = 256


def _cast_weight(w_ref, wb_ref):
    def chunk(i, carry):
        rows = pl.ds(pl.multiple_of(i * CAST_ROWS, CAST_ROWS), CAST_ROWS)
        wb_ref[rows, :] = w_ref[rows, :].astype(wb_ref.dtype)
        return carry

    lax.fori_loop(0, w_ref.shape[0] // CAST_ROWS, chunk, 0)


def _in_proj_kernel(a_ref, w_ref, o_ref, wb_ref):
    first = pl.program_id(1) == 0

    @pl.when(first)
    def _():
        acc = None
        for c in range(0, w_ref.shape[0], CAST_ROWS):
            wc = w_ref[c:c + CAST_ROWS, :].astype(BF16)
            wb_ref[c:c + CAST_ROWS, :] = wc
            part = jnp.dot(a_ref[:, c:c + CAST_ROWS], wc, preferred_element_type=F32)
            acc = part if acc is None else acc + part
        o_ref[...] = acc.astype(o_ref.dtype)

    @pl.when(jnp.logical_not(first))
    def _():
        o_ref[...] = jnp.dot(a_ref[...], wb_ref[...],
                             preferred_element_type=F32).astype(o_ref.dtype)


def _in_proj(a, w_all, layer, tm=2048, tn=1024):
    m, k = a.shape
    n = w_all.shape[2]
    return pl.pallas_call(
        _in_proj_kernel,
        grid=(n // tn, m // tm),
        in_specs=[pl.BlockSpec((tm, k), lambda j, i: (i, 0)),
                  pl.BlockSpec((None, k, tn), lambda j, i: (layer, 0, j))],
        out_specs=pl.BlockSpec((tm, tn), lambda j, i: (i, j)),
        out_shape=jax.ShapeDtypeStruct((m, n), BF16),
        scratch_shapes=[pltpu.VMEM((k, tn), BF16)],
        compiler_params=_params(("parallel", "arbitrary")),
        name="in_proj",
    )(a, w_all)


POOL_ROWS = 128
assert max(POOL_WINDOWS) - 1 <= POOL_ROWS


def _pool_kernel(z_ref, pw_ref, ps_ref, o_ref, pwb_ref, band_ref):
    s_len = z_ref.shape[0]
    width = len(POOL_WINDOWS) * POOL_CH
    ti = lax.broadcasted_iota(jnp.int32, (POOL_ROWS, 2 * POOL_ROWS), 0)
    kj = lax.broadcasted_iota(jnp.int32, (POOL_ROWS, 2 * POOL_ROWS), 1)
    dist = ti + POOL_ROWS - kj
    for g, w in enumerate(POOL_WINDOWS):
        pwb_ref[g] = pw_ref[g].astype(BF16)
        band_ref[g] = jnp.where((dist >= 0) & (dist < w), 1.0, 0.0).astype(BF16)

    for r in range(0, s_len, POOL_ROWS):
        for g, w in enumerate(POOL_WINDOWS):
            cs = slice(g * POOL_CH, (g + 1) * POOL_CH)
            xb = z_ref[r:r + POOL_ROWS, cs]
            x = xb.astype(F32)
            if r == 0:
                acc = jnp.dot(band_ref[g, :, POOL_ROWS:2 * POOL_ROWS], xb, preferred_element_type=F32)
            else:
                acc = jnp.dot(band_ref[g], z_ref[r - POOL_ROWS:r + POOL_ROWS, cs],
                              preferred_element_type=F32)
            if r + 1 >= w:
                cnt = float(w)
            else:
                row = r + lax.broadcasted_iota(jnp.int32, (POOL_ROWS, POOL_CH), 0)
                cnt = jnp.minimum(row + 1, w).astype(F32)
            pooled = acc / cnt - x
            mixed = jnp.dot(pooled.astype(BF16), pwb_ref[g], preferred_element_type=F32)
            gate = z_ref[r:r + POOL_ROWS, width + g * POOL_CH:width + (g + 1) * POOL_CH].astype(F32)
            y = mixed * ps_ref[:, cs] * _silu(gate)
            o_ref[r:r + POOL_ROWS, cs] = y.astype(o_ref.dtype)


def _pool_mixer(z3, pool_w, pool_scale):
    b, s, _ = z3.shape
    width = len(POOL_WINDOWS) * POOL_CH
    return pl.pallas_call(
        _pool_kernel,
        grid=(b,),
        in_specs=[pl.BlockSpec((None, s, 2 * width), lambda i: (i, 0, 0)),
                  pl.BlockSpec(pool_w.shape, lambda i: (0, 0, 0)),
                  pl.BlockSpec((1, width), lambda i: (0, 0))],
        out_specs=pl.BlockSpec((None, s, width), lambda i: (i, 0, 0)),
        out_shape=jax.ShapeDtypeStruct((b, s, width), BF16),
        scratch_shapes=[pltpu.VMEM(pool_w.shape, BF16),
                        pltpu.VMEM((len(POOL_WINDOWS), POOL_ROWS, 2 * POOL_ROWS), BF16)],
        compiler_params=_params(("parallel",)),
        name="pool_mixer",
    )(z3, pool_w, pool_scale.reshape(1, width))


ATT_SCALE_LOG2 = HEAD_DIM ** -0.5 * float(np.log2(np.e))
ATT_ROWS = 1024
ATT_PREP_ROWS = 2048
ATT_MAX_BATCH = 16


def _for_chunks(n, body):
    if n == 1:
        body(0, 0)
    else:
        lax.fori_loop(0, n, body, 0)


def _aligned(start, align):
    return start if isinstance(start, int) else pl.multiple_of(start, align)


def _attn_kernel(q0, k0, v0, q1, k1, v1, q2, k2, v2, gate_ref, cos_ref, sin_ref,
                 o_ref, qkd_ref, qkv_ref, bias_ref, s_ref, ms_ref, acc_ref, m_ref, l_ref):
    s_len = o_ref.shape[0]
    qkv_in = ((q0, k0, v0), (q1, k1, v1), (q2, k2, v2))

    src = lax.broadcasted_iota(jnp.int32, (HEAD_DIM, HEAD_DIM), 0)
    dst = lax.broadcasted_iota(jnp.int32, (HEAD_DIM, HEAD_DIM), 1)
    perm = (jnp.where((dst < ROT_HALF) & (src == dst + ROT_HALF), -1.0, 0.0)
            + jnp.where((dst >= ROT_HALF) & (dst < ROT_DIM) & (src == dst - ROT_HALF), 1.0, 0.0)
            ).astype(BF16)

    strided = [g for g, d in enumerate(DILATIONS) if d > 1]
    dense = [g for g, d in enumerate(DILATIONS) if d == 1]

    prep_rows, merge_rows = min(ATT_PREP_ROWS, s_len), min(ATT_ROWS, s_len)

    def prep(i, c):
        chunk = pl.ds(_aligned(i * prep_rows, prep_rows), prep_rows)
        cos, sin = cos_ref[chunk, :], sin_ref[chunk, :]
        for g, d in enumerate(DILATIONS):
            for j in range(2):
                t = qkv_in[g][j][chunk, :]
                partner = jnp.dot(t, perm, preferred_element_type=F32)
                roped = t.astype(F32) * cos + partner * sin
                if d == 1:
                    qkd_ref[2 * dense.index(g) + j, chunk, :] = roped.astype(BF16)
                else:
                    qkv_ref[3 * strided.index(g) + j, chunk, :] = roped
            if d > 1:
                qkv_ref[3 * strided.index(g) + 2, chunk, :] = qkv_in[g][2][chunk, :].astype(F32)
        return c

    _for_chunks(s_len // prep_rows, prep)

    qi = lax.broadcasted_iota(jnp.int32, (SPAN, 2 * SPAN), 0)
    kj = lax.broadcasted_iota(jnp.int32, (SPAN, 2 * SPAN), 1)
    band, causal = 0, 1
    bias_ref[band] = jnp.where((kj >= qi) & (kj <= qi + SPAN), 0.0, NEG)
    bias_ref[causal] = jnp.where(kj <= qi, 0.0, NEG)

    def rows(d, start, cnt):
        return pl.ds(_aligned(start, SPAN), cnt) if d == 1 else pl.ds(start, cnt, stride=d)

    def operand(g, d, j, start, cnt):
        if d == 1:
            src = qkv_in[g][2] if j == 2 else qkd_ref.at[2 * dense.index(g) + j]
            return src[rows(d, start, cnt), :]
        return qkv_ref[3 * strided.index(g) + j, rows(d, start, cnt), :].astype(BF16)

    def run_batch(blocks, n_keys, mask):
        assert len(blocks) <= ATT_MAX_BATCH

        def blk(j):
            return slice(j * SPAN, (j + 1) * SPAN)

        for j, (g, d, q_start, k_start) in enumerate(blocks):
            s = lax.dot_general(operand(g, d, 0, q_start, SPAN), operand(g, d, 1, k_start, n_keys),
                                (((1,), (1,)), ((), ())), preferred_element_type=F32)
            s = s * ATT_SCALE_LOG2 + bias_ref[mask, :, 0:n_keys]
            s_ref[blk(j), 0:n_keys] = s
            m = jnp.max(s, axis=-1, keepdims=True)
            ms_ref[blk(j), :] = jnp.broadcast_to(m, (SPAN, HEAD_DIM))

        for j, (g, d, q_start, k_start) in enumerate(blocks):
            m = ms_ref[blk(j), :]
            den, ps = None, []
            for c in range(n_keys // HEAD_DIM):
                p = jnp.exp2(s_ref[blk(j), c * HEAD_DIM:(c + 1) * HEAD_DIM] - m)
                ps.append(p.astype(BF16))
                den = p if den is None else den + p
            den = jnp.sum(den, axis=-1, keepdims=True)
            m_ref[g, rows(d, q_start, SPAN), :] = m
            l_ref[g, rows(d, q_start, SPAN), :] = jnp.broadcast_to(den, (SPAN, HEAD_DIM))
            pb = ps[0] if len(ps) == 1 else jnp.concatenate(ps, axis=1)
            acc_ref[g, rows(d, q_start, SPAN), :] = jnp.dot(
                pb, operand(g, d, 2, k_start, n_keys), preferred_element_type=F32)

    first_blocks = [(g, d, r, r) for g, d in enumerate(DILATIONS) for r in range(d)]
    for lo in range(0, len(first_blocks), ATT_MAX_BATCH):
        run_batch(first_blocks[lo:lo + ATT_MAX_BATCH], SPAN, causal)
    for g, d in enumerate(DILATIONS):
        stride_blk = d * SPAN
        band_blocks = [(g, d, r + i * stride_blk, r + (i - 1) * stride_blk)
                       for i in range(1, s_len // stride_blk) for r in range(d)]
        for lo in range(0, len(band_blocks), ATT_MAX_BATCH):
            run_batch(band_blocks[lo:lo + ATT_MAX_BATCH], 2 * SPAN, band)

    def merge(i, c):
        rows = pl.ds(_aligned(i * merge_rows, merge_rows), merge_rows)
        ms = [m_ref[g, rows, :] for g in range(ATT_GROUPS)]
        m = functools.reduce(jnp.maximum, ms)
        num = den = None
        for g in range(ATT_GROUPS):
            e = jnp.exp2(ms[g] - m)
            ng, dg = e * acc_ref[g, rows, :], e * l_ref[g, rows, :]
            num, den = (ng, dg) if num is None else (num + ng, den + dg)
        gate = gate_ref[rows, :].astype(F32)
        o_ref[rows, :] = (num / den * _silu(gate)).astype(o_ref.dtype)
        return c

    _for_chunks(s_len // merge_rows, merge)


def _rope_tables(s_len):
    pos = jnp.arange(s_len, dtype=F32)
    inv_freq = jnp.power(ROPE_THETA, -jnp.arange(0, ROT_DIM, 2, dtype=F32) / ROT_DIM)
    ang = pos[:, None] * inv_freq[None, :]
    cos, sin = jnp.cos(ang), jnp.sin(ang)
    rest = jnp.zeros((s_len, HEAD_DIM - ROT_DIM), F32)
    return (jnp.concatenate([cos, cos, rest + 1.0], axis=-1),
            jnp.concatenate([sin, sin, rest], axis=-1))


def _dilated_attention(z3, q_col, n_heads):
    b, s, _ = z3.shape
    per = ATT_GROUPS * n_heads
    qb = q_col // HEAD_DIM
    cos_t, sin_t = _rope_tables(s)

    def head_spec(blk0):
        return pl.BlockSpec((None, s, HEAD_DIM), lambda i, h, blk0=blk0: (i, 0, blk0 + h))

    in_specs = []
    for g in range(ATT_GROUPS):
        for j in range(3):
            in_specs.append(head_spec(qb + j * per + g * n_heads))
    in_specs.append(head_spec(qb + 3 * per))
    tab_spec = pl.BlockSpec((s, HEAD_DIM), lambda i, h: (0, 0))
    in_specs += [tab_spec, tab_spec]
    return pl.pallas_call(
        _attn_kernel,
        grid=(b, n_heads),
        in_specs=in_specs,
        out_specs=pl.BlockSpec((None, s, HEAD_DIM), lambda i, h: (i, 0, h)),
        out_shape=jax.ShapeDtypeStruct((b, s, n_heads * HEAD_DIM), BF16),
        scratch_shapes=[pltpu.VMEM((2 * sum(d == 1 for d in DILATIONS), s, HEAD_DIM), BF16),
                        pltpu.VMEM((3 * sum(d > 1 for d in DILATIONS), s, HEAD_DIM), F32),
                        pltpu.VMEM((2, SPAN, 2 * SPAN), F32),
                        pltpu.VMEM((ATT_MAX_BATCH * SPAN, 2 * SPAN), F32),
                        pltpu.VMEM((ATT_MAX_BATCH * SPAN, HEAD_DIM), F32),
                        pltpu.VMEM((ATT_GROUPS, s, HEAD_DIM), F32),
                        pltpu.VMEM((ATT_GROUPS, s, HEAD_DIM), F32),
                        pltpu.VMEM((ATT_GROUPS, s, HEAD_DIM), F32)],
        compiler_params=_params(("parallel", "parallel")),
        name="dilated_attention",
    )(*([z3] * 10), cos_t, sin_t)


OUT_SUB_ROWS = 256


def _out_proj_kernel(*refs, with_next, conv_in):
    refs = list(refs)
    ya_ref = refs.pop(0)
    if conv_in:
        conv_ref, gate_ref, ng_ref, nb_ref = refs[:4]
        del refs[:4]
    else:
        yb_ref = refs.pop(0)
    w_ref, x_ref, gpost_ref = refs[:3]
    del refs[:3]
    if with_next:
        gnext_ref, xo_ref, ho_ref, wb_ref = refs
    else:
        xo_ref, wb_ref = refs

    @pl.when(pl.program_id(0) == 0)
    def _():
        _cast_weight(w_ref, wb_ref)

    half = ya_ref.shape[1]
    for r in range(0, ya_ref.shape[0], OUT_SUB_ROWS):
        rows = slice(r, r + OUT_SUB_ROWS)
        if conv_in:
            c = conv_ref[rows, :]
            mu = jnp.mean(c, axis=-1, keepdims=True)
            cc = c - mu
            var = jnp.mean(cc * cc, axis=-1, keepdims=True)
            yb = _silu(cc * lax.rsqrt(var + EPS) * ng_ref[...] + nb_ref[...])
            yb = (yb * _silu(gate_ref[rows, :].astype(F32))).astype(BF16)
        else:
            yb = yb_ref[rows, :]
        y = jnp.dot(ya_ref[rows, :], wb_ref[0:half, :], preferred_element_type=F32)
        y = y + jnp.dot(yb, wb_ref[half:2 * half, :], preferred_element_type=F32)
        yn = y * lax.rsqrt(jnp.mean(y * y, axis=-1, keepdims=True) + EPS) * gpost_ref[...]
        xn = x_ref[rows, :] + yn
        xo_ref[rows, :] = xn
        if with_next:
            hn = xn * lax.rsqrt(jnp.mean(xn * xn, axis=-1, keepdims=True) + EPS) * gnext_ref[...]
            ho_ref[rows, :] = hn.astype(ho_ref.dtype)


def _out_proj(ya, yb, w_all, layer, x, g_post, g_next=None, tm=512):
    t, half = ya.shape
    _, k, d = w_all.shape
    with_next = g_next is not None
    conv_in = isinstance(yb, tuple)
    row = lambda i: (i, 0)
    fixed = lambda i: (0, 0)
    in_specs = [pl.BlockSpec((tm, half), row)]
    args = [ya]
    if conv_in:
        conv, z, gate_blk, cn_g, cn_b = yb
        in_specs += [pl.BlockSpec((tm, half), row),
                     pl.BlockSpec((tm, half), lambda i: (i, gate_blk)),
                     pl.BlockSpec((1, half), fixed), pl.BlockSpec((1, half), fixed)]
        args += [conv, z, cn_g.reshape(1, half), cn_b.reshape(1, half)]
    else:
        in_specs.append(pl.BlockSpec((tm, half), row))
        args.append(yb)
    in_specs += [pl.BlockSpec((None, k, d), lambda i: (layer, 0, 0), pipeline_mode=pl.Buffered(1)),
                 pl.BlockSpec((tm, d), row),
                 pl.BlockSpec((1, d), fixed)]
    args += [w_all, x, g_post.reshape(1, d)]
    out_specs = [pl.BlockSpec((tm, d), row)]
    out_shape = [jax.ShapeDtypeStruct((t, d), F32)]
    if with_next:
        in_specs.append(pl.BlockSpec((1, d), fixed))
        args.append(g_next.reshape(1, d))
        out_specs.append(pl.BlockSpec((tm, d), row))
        out_shape.append(jax.ShapeDtypeStruct((t, d), BF16))
    return pl.pallas_call(
        functools.partial(_out_proj_kernel, with_next=with_next, conv_in=conv_in),
        grid=(t // tm,),
        in_specs=in_specs,
        out_specs=out_specs,
        out_shape=out_shape,
        scratch_shapes=[pltpu.VMEM((k, d), BF16)],
        compiler_params=_params(("arbitrary",)),
        name="out_proj",
    )(*args)


def _sgu_kernel(u_ref, v_ref, cg_ref, g_ref, b_ref, ws_ref, bias_ref, o_ref):
    tm = u_ref.shape[0]
    ii = lax.broadcasted_iota(jnp.int32, (CHUNK, CHUNK), 0)
    jj = lax.broadcasted_iota(jnp.int32, (CHUNK, CHUNK), 1)
    tril = jj <= ii
    for c in range(tm // CHUNK):
        rows = slice(c * CHUNK, (c + 1) * CHUNK)
        v = v_ref[rows, :].astype(F32)
        mu = jnp.mean(v, axis=-1, keepdims=True)
        vc = v - mu
        var = jnp.mean(vc * vc, axis=-1, keepdims=True)
        vn = (vc * lax.rsqrt(var + EPS) * g_ref[...] + b_ref[...]).astype(BF16)
        for h in range(SGU_GROUPS):
            cs = slice(h * SGU_CH, (h + 1) * SGU_CH)
            wm = jnp.where(tril, ws_ref[h], 0.0).astype(BF16)
            sg = jnp.dot(wm, vn[:, cs], preferred_element_type=F32) + bias_ref[:, cs]
            y = u_ref[rows, cs].astype(F32) * sg * _silu(cg_ref[rows, cs].astype(F32))
            o_ref[rows, cs] = y.astype(o_ref.dtype)


def _sgu(z, sgu_g, sgu_b, sgu_w, sgu_bias, tm=1024):
    t = z.shape[0]
    width = SGU_GROUPS * SGU_CH
    bias_full = jnp.repeat(sgu_bias.T, SGU_CH, axis=1)
    fixed2 = lambda i: (0, 0)
    return pl.pallas_call(
        _sgu_kernel,
        grid=(t // tm,),
        in_specs=[pl.BlockSpec((tm, width), lambda i: (i, 0)),
                  pl.BlockSpec((tm, width), lambda i: (i, 1)),
                  pl.BlockSpec((tm, width), lambda i: (i, 2)),
                  pl.BlockSpec((1, width), fixed2), pl.BlockSpec((1, width), fixed2),
                  pl.BlockSpec(sgu_w.shape, lambda i: (0, 0, 0)),
                  pl.BlockSpec((CHUNK, width), fixed2)],
        out_specs=pl.BlockSpec((tm, width), lambda i: (i, 0)),
        out_shape=jax.ShapeDtypeStruct((t, width), BF16),
        compiler_params=_params(("parallel",)),
        name="sgu",
    )(z, z, z, sgu_g.reshape(1, width), sgu_b.reshape(1, width), sgu_w, bias_full)


CONV_PAD = 32
CONV_ROWS = 128
CONV_FILL = 128
CONV_SLABS = 2


def _conv_kernel(dv_ref, dg_ref, w_ref, cb_ref, o_ref, dpad_ref):
    s_len = dv_ref.shape[0]
    for c in range(CONV_SLABS):
        dpad_ref[c, 0:CONV_PAD, :] = jnp.zeros((CONV_PAD, LANES), F32)

    def fill(i, carry):
        r = pl.multiple_of(i * CONV_FILL, CONV_FILL)
        rows = pl.ds(r, CONV_FILL)
        d = dv_ref[rows, :].astype(F32) * jax.nn.sigmoid(dg_ref[rows, :].astype(F32))
        for c in range(CONV_SLABS):
            dpad_ref[c, pl.ds(CONV_PAD + r, CONV_FILL), :] = d[:, c * LANES:(c + 1) * LANES]
        return carry

    lax.fori_loop(0, s_len // CONV_FILL, fill, 0)

    lead = CONV_PAD - (CONV_K - 1)
    for c in range(CONV_SLABS):
        cs = slice(c * LANES, (c + 1) * LANES)
        taps = [w_ref[k:k + 1, cs] for k in range(CONV_K)]
        for r in range(0, s_len, CONV_ROWS):
            acc = dpad_ref[c, r + lead:r + lead + CONV_ROWS, :] * taps[0]
            for k in range(1, CONV_K):
                acc = acc + dpad_ref[c, r + lead + k:r + lead + k + CONV_ROWS, :] * taps[k]
            o_ref[r:r + CONV_ROWS, cs] = acc + cb_ref[:, cs]


def _conv_taps(z3, col, conv_w, conv_b):
    b, s, cols = z3.shape
    ch = conv_w.shape[1]
    width = CONV_SLABS * LANES
    n_steps = ch // width
    blk = col // width
    vec = lambda a: a.reshape(1, ch)
    return pl.pallas_call(
        _conv_kernel,
        grid=(b, n_steps),
        in_specs=[pl.BlockSpec((None, s, width), lambda i, c: (i, 0, blk + c)),
                  pl.BlockSpec((None, s, width), lambda i, c: (i, 0, blk + n_steps + c)),
                  pl.BlockSpec((CONV_K, width), lambda i, c: (0, c)),
                  pl.BlockSpec((1, width), lambda i, c: (0, c))],
        out_specs=pl.BlockSpec((None, s, width), lambda i, c: (i, 0, c)),
        out_shape=jax.ShapeDtypeStruct((b, s, ch), F32),
        scratch_shapes=[pltpu.VMEM((CONV_SLABS, CONV_PAD + s, LANES), F32)],
        compiler_params=_params(("parallel", "parallel")),
        name="conv_taps",
    )(z3, z3, conv_w, vec(conv_b))


def kernel(x, e_pre_norm, e_w_in, e_pool_w, e_pool_scale, e_w_out, e_post_norm, o_pre_norm, o_w_in, o_sgu_norm_g, o_sgu_norm_b, o_sgu_w, o_sgu_b, o_conv_w, o_conv_b, o_conv_norm_g, o_conv_norm_b, o_w_out, o_post_norm):
    b, s, d = x.shape
    t = b * s
    n_even, n_odd = e_w_in.shape[0], o_w_in.shape[0]
    depth = n_even + n_odd
    pool_width = len(POOL_WINDOWS) * POOL_CH
    n_heads = (e_w_out.shape[1] - pool_width) // HEAD_DIM
    sgu_width = SGU_GROUPS * SGU_CH

    def pre_gain(i):
        return e_pre_norm[i // 2] if i % 2 == 0 else o_pre_norm[i // 2]

    xf = x.reshape(t, d)
    h = _rmsnorm(xf, pre_gain(0))
    for i in range(depth):
        j = i // 2
        g_next = pre_gain(i + 1) if i + 1 < depth else None
        if i % 2 == 0:
            z = _in_proj(h, e_w_in, j)
            z3 = z.reshape(b, s, z.shape[1])
            ya = _pool_mixer(z3, e_pool_w[j], e_pool_scale[j])
            yb = _dilated_attention(z3, 2 * pool_width, n_heads)
            w_out, g_post = e_w_out, e_post_norm[j]
        else:
            z = _in_proj(h, o_w_in, j)
            z3 = z.reshape(b, s, z.shape[1])
            ya = _sgu(z, o_sgu_norm_g[j], o_sgu_norm_b[j], o_sgu_w[j], o_sgu_b[j])
            conv_width = o_conv_w.shape[2]
            conv = _conv_taps(z3, 3 * sgu_width, o_conv_w[j], o_conv_b[j])
            gate_blk = (3 * sgu_width + 2 * conv_width) // conv_width
            yb = (conv.reshape(t, conv_width), z, gate_blk, o_conv_norm_g[j], o_conv_norm_b[j])
            w_out, g_post = o_w_out, o_post_norm[j]
        if not isinstance(yb, tuple):
            yb = yb.reshape(t, -1)
        outs = _out_proj(ya.reshape(t, -1), yb, w_out, j, xf, g_post, g_next)
        if g_next is None:
            xf = outs[0]
        else:
            xf, h = outs
    return xf.reshape(b, s, d)
```

```python
import functools

import jax
import jax.numpy as jnp
import numpy as np
from jax import lax
from jax.experimental import pallas as pl
from jax.experimental.pallas import tpu as pltpu

F32 = jnp.float32
BF16 = jnp.bfloat16

EPS = 1e-6
NEG = -1e30
HEAD_DIM = 128
ROT_DIM = HEAD_DIM // 4
ROT_HALF = ROT_DIM // 2
ROPE_THETA = 500000.0
DILATIONS = (1, 4, 16)
SPAN = 128
ATT_GROUPS = len(DILATIONS)
POOL_WINDOWS = (2, 4, 8, 16)
POOL_CH = 256
SGU_GROUPS = 4
SGU_CH = 256
CHUNK = 128
CONV_K = 31

LANES = 128
VMEM_LIMIT = 56 * 1024 * 1024


def _params(sem, vmem=VMEM_LIMIT):
    return pltpu.CompilerParams(dimension_semantics=sem, vmem_limit_bytes=vmem)


def _silu(x):
    return x * jax.nn.sigmoid(x)


def _rmsnorm_kernel(x_ref, g_ref, o_ref):
    x = x_ref[...]
    ms = jnp.mean(x * x, axis=-1, keepdims=True)
    o_ref[...] = (x * lax.rsqrt(ms + EPS) * g_ref[...]).astype(o_ref.dtype)


def _rmsnorm(x, g, tm=1024):
    t, d = x.shape
    return pl.pallas_call(
        _rmsnorm_kernel,
        grid=(t // tm,),
        in_specs=[pl.BlockSpec((tm, d), lambda i: (i, 0)),
                  pl.BlockSpec((1, d), lambda i: (0, 0))],
        out_specs=pl.BlockSpec((tm, d), lambda i: (i, 0)),
        out_shape=jax.ShapeDtypeStruct((t, d), BF16),
        compiler_params=_params(("parallel",)),
        name="rmsnorm",
    )(x, g.reshape(1, d))


CAST_ROWS = 256


def _cast_weight(w_ref, wb_ref):
    def chunk(i, carry):
        rows = pl.ds(pl.multiple_of(i * CAST_ROWS, CAST_ROWS), CAST_ROWS)
        wb_ref[rows, :] = w_ref[rows, :].astype(wb_ref.dtype)
        return carry

    lax.fori_loop(0, w_ref.shape[0] // CAST_ROWS, chunk, 0)


def _in_proj_kernel(a_ref, w_ref, o_ref, wb_ref):
    first = pl.program_id(1) == 0

    @pl.when(first)
    def _():
        acc = None
        for c in range(0, w_ref.shape[0], CAST_ROWS):
            wc = w_ref[c:c + CAST_ROWS, :].astype(BF16)
            wb_ref[c:c + CAST_ROWS, :] = wc
            part = jnp.dot(a_ref[:, c:c + CAST_ROWS], wc, preferred_element_type=F32)
            acc = part if acc is None else acc + part
        o_ref[...] = acc.astype(o_ref.dtype)

    @pl.when(jnp.logical_not(first))
    def _():
        o_ref[...] = jnp.dot(a_ref[...], wb_ref[...],
                             preferred_element_type=F32).astype(o_ref.dtype)


def _in_proj(a, w_all, layer, tm=2048, tn=1024):
    m, k = a.shape
    n = w_all.shape[2]
    return pl.pallas_call(
        _in_proj_kernel,
        grid=(n // tn, m // tm),
        in_specs=[pl.BlockSpec((tm, k), lambda j, i: (i, 0)),
                  pl.BlockSpec((None, k, tn), lambda j, i: (layer, 0, j))],
        out_specs=pl.BlockSpec((tm, tn), lambda j, i: (i, j)),
        out_shape=jax.ShapeDtypeStruct((m, n), BF16),
        scratch_shapes=[pltpu.VMEM((k, tn), BF16)],
        compiler_params=_params(("parallel", "arbitrary")),
        name="in_proj",
    )(a, w_all)


POOL_ROWS = 128
assert max(POOL_WINDOWS) - 1 <= POOL_ROWS


def _pool_kernel(z_ref, pw_ref, ps_ref, o_ref, pwb_ref, band_ref):
    s_len = z_ref.shape[0]
    width = len(POOL_WINDOWS) * POOL_CH
    ti = lax.broadcasted_iota(jnp.int32, (POOL_ROWS, 2 * POOL_ROWS), 0)
    kj = lax.broadcasted_iota(jnp.int32, (POOL_ROWS, 2 * POOL_ROWS), 1)
    dist = ti + POOL_ROWS - kj
    for g, w in enumerate(POOL_WINDOWS):
        pwb_ref[g] = pw_ref[g].astype(BF16)
        band_ref[g] = jnp.where((dist >= 0) & (dist < w), 1.0, 0.0).astype(BF16)

    for r in range(0, s_len, POOL_ROWS):
        for g, w in enumerate(POOL_WINDOWS):
            cs = slice(g * POOL_CH, (g + 1) * POOL_CH)
            xb = z_ref[r:r + POOL_ROWS, cs]
            x = xb.astype(F32)
            if r == 0:
                acc = jnp.dot(band_ref[g, :, POOL_ROWS:2 * POOL_ROWS], xb, preferred_element_type=F32)
            else:
                acc = jnp.dot(band_ref[g], z_ref[r - POOL_ROWS:r + POOL_ROWS, cs],
                              preferred_element_type=F32)
            if r + 1 >= w:
                cnt = float(w)
            else:
                row = r + lax.broadcasted_iota(jnp.int32, (POOL_ROWS, POOL_CH), 0)
                cnt = jnp.minimum(row + 1, w).astype(F32)
            pooled = acc / cnt - x
            mixed = jnp.dot(pooled.astype(BF16), pwb_ref[g], preferred_element_type=F32)
            gate = z_ref[r:r + POOL_ROWS, width + g * POOL_CH:width + (g + 1) * POOL_CH].astype(F32)
            y = mixed * ps_ref[:, cs] * _silu(gate)
            o_ref[r:r + POOL_ROWS, cs] = y.astype(o_ref.dtype)


def _pool_mixer(z3, pool_w, pool_scale):
    b, s, _ = z3.shape
    width = len(POOL_WINDOWS) * POOL_CH
    return pl.pallas_call(
        _pool_kernel,
        grid=(b,),
        in_specs=[pl.BlockSpec((None, s, 2 * width), lambda i: (i, 0, 0)),
                  pl.BlockSpec(pool_w.shape, lambda i: (0, 0, 0)),
                  pl.BlockSpec((1, width), lambda i: (0, 0))],
        out_specs=pl.BlockSpec((None, s, width), lambda i: (i, 0, 0)),
        out_shape=jax.ShapeDtypeStruct((b, s, width), BF16),
        scratch_shapes=[pltpu.VMEM(pool_w.shape, BF16),
                        pltpu.VMEM((len(POOL_WINDOWS), POOL_ROWS, 2 * POOL_ROWS), BF16)],
        compiler_params=_params(("parallel",)),
        name="pool_mixer",
    )(z3, pool_w, pool_scale.reshape(1, width))


ATT_SCALE_LOG2 = HEAD_DIM ** -0.5 * float(np.log2(np.e))
ATT_ROWS = 1024
ATT_PREP_ROWS = 2048
ATT_MAX_BATCH = 16


def _for_chunks(n, body):
    if n == 1:
        body(0, 0)
    else:
        lax.fori_loop(0, n, body, 0)


def _aligned(start, align):
    return start if isinstance(start, int) else pl.multiple_of(start, align)


def _attn_kernel(q0, k0, v0, q1, k1, v1, q2, k2, v2, gate_ref, cos_ref, sin_ref,
                 o_ref, qkd_ref, qkv_ref, bias_ref, s_ref, ms_ref, acc_ref, m_ref, l_ref):
    s_len = o_ref.shape[0]
    qkv_in = ((q0, k0, v0), (q1, k1, v1), (q2, k2, v2))

    src = lax.broadcasted_iota(jnp.int32, (HEAD_DIM, HEAD_DIM), 0)
    dst = lax.broadcasted_iota(jnp.int32, (HEAD_DIM, HEAD_DIM), 1)
    perm = (jnp.where((dst < ROT_HALF) & (src == dst + ROT_HALF), -1.0, 0.0)
            + jnp.where((dst >= ROT_HALF) & (dst < ROT_DIM) & (src == dst - ROT_HALF), 1.0, 0.0)
            ).astype(BF16)

    strided = [g for g, d in enumerate(DILATIONS) if d > 1]
    dense = [g for g, d in enumerate(DILATIONS) if d == 1]

    prep_rows, merge_rows = min(ATT_PREP_ROWS, s_len), min(ATT_ROWS, s_len)

    def prep(i, c):
        chunk = pl.ds(_aligned(i * prep_rows, prep_rows), prep_rows)
        cos, sin = cos_ref[chunk, :], sin_ref[chunk, :]
        for g, d in enumerate(DILATIONS):
            for j in range(2):
                t = qkv_in[g][j][chunk, :]
                partner = jnp.dot(t, perm, preferred_element_type=F32)
                roped = t.astype(F32) * cos + partner * sin
                if d == 1:
                    qkd_ref[2 * dense.index(g) + j, chunk, :] = roped.astype(BF16)
                else:
                    qkv_ref[3 * strided.index(g) + j, chunk, :] = roped
            if d > 1:
                qkv_ref[3 * strided.index(g) + 2, chunk, :] = qkv_in[g][2][chunk, :].astype(F32)
        return c

    _for_chunks(s_len // prep_rows, prep)

    qi = lax.broadcasted_iota(jnp.int32, (SPAN, 2 * SPAN), 0)
    kj = lax.broadcasted_iota(jnp.int32, (SPAN, 2 * SPAN), 1)
    band, causal = 0, 1
    bias_ref[band] = jnp.where((kj >= qi) & (kj <= qi + SPAN), 0.0, NEG)
    bias_ref[causal] = jnp.where(kj <= qi, 0.0, NEG)

    def rows(d, start, cnt):
        return pl.ds(_aligned(start, SPAN), cnt) if d == 1 else pl.ds(start, cnt, stride=d)

    def operand(g, d, j, start, cnt):
        if d == 1:
            src = qkv_in[g][2] if j == 2 else qkd_ref.at[2 * dense.index(g) + j]
            return src[rows(d, start, cnt), :]
        return qkv_ref[3 * strided.index(g) + j, rows(d, start, cnt), :].astype(BF16)

    def run_batch(blocks, n_keys, mask):
        assert len(blocks) <= ATT_MAX_BATCH

        def blk(j):
            return slice(j * SPAN, (j + 1) * SPAN)

        for j, (g, d, q_start, k_start) in enumerate(blocks):
            s = lax.dot_general(operand(g, d, 0, q_start, SPAN), operand(g, d, 1, k_start, n_keys),
                                (((1,), (1,)), ((), ())), preferred_element_type=F32)
            s = s * ATT_SCALE_LOG2 + bias_ref[mask, :, 0:n_keys]
            s_ref[blk(j), 0:n_keys] = s
            m = jnp.max(s, axis=-1, keepdims=True)
            ms_ref[blk(j), :] = jnp.broadcast_to(m, (SPAN, HEAD_DIM))

        for j, (g, d, q_start, k_start) in enumerate(blocks):
            m = ms_ref[blk(j), :]
            den, ps = None, []
            for c in range(n_keys // HEAD_DIM):
                p = jnp.exp2(s_ref[blk(j), c * HEAD_DIM:(c + 1) * HEAD_DIM] - m)
                ps.append(p.astype(BF16))
                den = p if den is None else den + p
            den = jnp.sum(den, axis=-1, keepdims=True)
            m_ref[g, rows(d, q_start, SPAN), :] = m
            l_ref[g, rows(d, q_start, SPAN), :] = jnp.broadcast_to(den, (SPAN, HEAD_DIM))
            pb = ps[0] if len(ps) == 1 else jnp.concatenate(ps, axis=1)
            acc_ref[g, rows(d, q_start, SPAN), :] = jnp.dot(
                pb, operand(g, d, 2, k_start, n_keys), preferred_element_type=F32)

    first_blocks = [(g, d, r, r) for g, d in enumerate(DILATIONS) for r in range(d)]
    for lo in range(0, len(first_blocks), ATT_MAX_BATCH):
        run_batch(first_blocks[lo:lo + ATT_MAX_BATCH], SPAN, causal)
    for g, d in enumerate(DILATIONS):
        stride_blk = d * SPAN
        band_blocks = [(g, d, r + i * stride_blk, r + (i - 1) * stride_blk)
                       for i in range(1, s_len // stride_blk) for r in range(d)]
        for lo in range(0, len(band_blocks), ATT_MAX_BATCH):
            run_batch(band_blocks[lo:lo + ATT_MAX_BATCH], 2 * SPAN, band)

    def merge(i, c):
        rows = pl.ds(_aligned(i * merge_rows, merge_rows), merge_rows)
        ms = [m_ref[g, rows, :] for g in range(ATT_GROUPS)]
        m = functools.reduce(jnp.maximum, ms)
        num = den = None
        for g in range(ATT_GROUPS):
            e = jnp.exp2(ms[g] - m)
            ng, dg = e * acc_ref[g, rows, :], e * l_ref[g, rows, :]
            num, den = (ng, dg) if num is None else (num + ng, den + dg)
        gate = gate_ref[rows, :].astype(F32)
        o_ref[rows, :] = (num / den * _silu(gate)).astype(o_ref.dtype)
        return c

    _for_chunks(s_len // merge_rows, merge)


def _rope_tables(s_len):
    pos = jnp.arange(s_len, dtype=F32)
    inv_freq = jnp.power(ROPE_THETA, -jnp.arange(0, ROT_DIM, 2, dtype=F32) / ROT_DIM)
    ang = pos[:, None] * inv_freq[None, :]
    cos, sin = jnp.cos(ang), jnp.sin(ang)
    rest = jnp.zeros((s_len, HEAD_DIM - ROT_DIM), F32)
    return (jnp.concatenate([cos, cos, rest + 1.0], axis=-1),
            jnp.concatenate([sin, sin, rest], axis=-1))


def _dilated_attention(z3, q_col, n_heads):
    b, s, _ = z3.shape
    per = ATT_GROUPS * n_heads
    qb = q_col // HEAD_DIM
    cos_t, sin_t = _rope_tables(s)

    def head_spec(blk0):
        return pl.BlockSpec((None, s, HEAD_DIM), lambda i, h, blk0=blk0: (i, 0, blk0 + h))

    in_specs = []
    for g in range(ATT_GROUPS):
        for j in range(3):
            in_specs.append(head_spec(qb + j * per + g * n_heads))
    in_specs.append(head_spec(qb + 3 * per))
    tab_spec = pl.BlockSpec((s, HEAD_DIM), lambda i, h: (0, 0))
    in_specs += [tab_spec, tab_spec]
    return pl.pallas_call(
        _attn_kernel,
        grid=(b, n_heads),
        in_specs=in_specs,
        out_specs=pl.BlockSpec((None, s, HEAD_DIM), lambda i, h: (i, 0, h)),
        out_shape=jax.ShapeDtypeStruct((b, s, n_heads * HEAD_DIM), BF16),
        scratch_shapes=[pltpu.VMEM((2 * sum(d == 1 for d in DILATIONS), s, HEAD_DIM), BF16),
                        pltpu.VMEM((3 * sum(d > 1 for d in DILATIONS), s, HEAD_DIM), F32),
                        pltpu.VMEM((2, SPAN, 2 * SPAN), F32),
                        pltpu.VMEM((ATT_MAX_BATCH * SPAN, 2 * SPAN), F32),
                        pltpu.VMEM((ATT_MAX_BATCH * SPAN, HEAD_DIM), F32),
                        pltpu.VMEM((ATT_GROUPS, s, HEAD_DIM), F32),
                        pltpu.VMEM((ATT_GROUPS, s, HEAD_DIM), F32),
                        pltpu.VMEM((ATT_GROUPS, s, HEAD_DIM), F32)],
        compiler_params=_params(("parallel", "parallel")),
        name="dilated_attention",
    )(*([z3] * 10), cos_t, sin_t)


OUT_SUB_ROWS = 256


def _out_proj_kernel(*refs, with_next, conv_in):
    refs = list(refs)
    ya_ref = refs.pop(0)
    if conv_in:
        conv_ref, gate_ref, ng_ref, nb_ref = refs[:4]
        del refs[:4]
    else:
        yb_ref = refs.pop(0)
    w_ref, x_ref, gpost_ref = refs[:3]
    del refs[:3]
    if with_next:
        gnext_ref, xo_ref, ho_ref, wb_ref = refs
    else:
        xo_ref, wb_ref = refs

    @pl.when(pl.program_id(0) == 0)
    def _():
        _cast_weight(w_ref, wb_ref)

    half = ya_ref.shape[1]
    for r in range(0, ya_ref.shape[0], OUT_SUB_ROWS):
        rows = slice(r, r + OUT_SUB_ROWS)
        if conv_in:
            c = conv_ref[rows, :]
            mu = jnp.mean(c, axis=-1, keepdims=True)
            cc = c - mu
            var = jnp.mean(cc * cc, axis=-1, keepdims=True)
            yb = _silu(cc * lax.rsqrt(var + EPS) * ng_ref[...] + nb_ref[...])
            yb = (yb * _silu(gate_ref[rows, :].astype(F32))).astype(BF16)
        else:
            yb = yb_ref[rows, :]
        y = jnp.dot(ya_ref[rows, :], wb_ref[0:half, :], preferred_element_type=F32)
        y = y + jnp.dot(yb, wb_ref[half:2 * half, :], preferred_element_type=F32)
        yn = y * lax.rsqrt(jnp.mean(y * y, axis=-1, keepdims=True) + EPS) * gpost_ref[...]
        xn = x_ref[rows, :] + yn
        xo_ref[rows, :] = xn
        if with_next:
            hn = xn * lax.rsqrt(jnp.mean(xn * xn, axis=-1, keepdims=True) + EPS) * gnext_ref[...]
            ho_ref[rows, :] = hn.astype(ho_ref.dtype)


def _out_proj(ya, yb, w_all, layer, x, g_post, g_next=None, tm=512):
    t, half = ya.shape
    _, k, d = w_all.shape
    with_next = g_next is not None
    conv_in = isinstance(yb, tuple)
    row = lambda i: (i, 0)
    fixed = lambda i: (0, 0)
    in_specs = [pl.BlockSpec((tm, half), row)]
    args = [ya]
    if conv_in:
        conv, z, gate_blk, cn_g, cn_b = yb
        in_specs += [pl.BlockSpec((tm, half), row),
                     pl.BlockSpec((tm, half), lambda i: (i, gate_blk)),
                     pl.BlockSpec((1, half), fixed), pl.BlockSpec((1, half), fixed)]
        args += [conv, z, cn_g.reshape(1, half), cn_b.reshape(1, half)]
    else:
        in_specs.append(pl.BlockSpec((tm, half), row))
        args.append(yb)
    in_specs += [pl.BlockSpec((None, k, d), lambda i: (layer, 0, 0), pipeline_mode=pl.Buffered(1)),
                 pl.BlockSpec((tm, d), row),
                 pl.BlockSpec((1, d), fixed)]
    args += [w_all, x, g_post.reshape(1, d)]
    out_specs = [pl.BlockSpec((tm, d), row)]
    out_shape = [jax.ShapeDtypeStruct((t, d), F32)]
    if with_next:
        in_specs.append(pl.BlockSpec((1, d), fixed))
        args.append(g_next.reshape(1, d))
        out_specs.append(pl.BlockSpec((tm, d), row))
        out_shape.append(jax.ShapeDtypeStruct((t, d), BF16))
    return pl.pallas_call(
        functools.partial(_out_proj_kernel, with_next=with_next, conv_in=conv_in),
        grid=(t // tm,),
        in_specs=in_specs,
        out_specs=out_specs,
        out_shape=out_shape,
        scratch_shapes=[pltpu.VMEM((k, d), BF16)],
        compiler_params=_params(("arbitrary",)),
        name="out_proj",
    )(*args)


def _sgu_kernel(u_ref, v_ref, cg_ref, g_ref, b_ref, ws_ref, bias_ref, o_ref):
    tm = u_ref.shape[0]
    ii = lax.broadcasted_iota(jnp.int32, (CHUNK, CHUNK), 0)
    jj = lax.broadcasted_iota(jnp.int32, (CHUNK, CHUNK), 1)
    tril = jj <= ii
    for c in range(tm // CHUNK):
        rows = slice(c * CHUNK, (c + 1) * CHUNK)
        v = v_ref[rows, :].astype(F32)
        mu = jnp.mean(v, axis=-1, keepdims=True)
        vc = v - mu
        var = jnp.mean(vc * vc, axis=-1, keepdims=True)
        vn = (vc * lax.rsqrt(var + EPS) * g_ref[...] + b_ref[...]).astype(BF16)
        for h in range(SGU_GROUPS):
            cs = slice(h * SGU_CH, (h + 1) * SGU_CH)
            wm = jnp.where(tril, ws_ref[h], 0.0).astype(BF16)
            sg = jnp.dot(wm, vn[:, cs], preferred_element_type=F32) + bias_ref[:, cs]
            y = u_ref[rows, cs].astype(F32) * sg * _silu(cg_ref[rows, cs].astype(F32))
            o_ref[rows, cs] = y.astype(o_ref.dtype)


def _sgu(z, sgu_g, sgu_b, sgu_w, sgu_bias, tm=1024):
    t = z.shape[0]
    width = SGU_GROUPS * SGU_CH
    bias_full = jnp.repeat(sgu_bias.T, SGU_CH, axis=1)
    fixed2 = lambda i: (0, 0)
    return pl.pallas_call(
        _sgu_kernel,
        grid=(t // tm,),
        in_specs=[pl.BlockSpec((tm, width), lambda i: (i, 0)),
                  pl.BlockSpec((tm, width), lambda i: (i, 1)),
                  pl.BlockSpec((tm, width), lambda i: (i, 2)),
                  pl.BlockSpec((1, width), fixed2), pl.BlockSpec((1, width), fixed2),
                  pl.BlockSpec(sgu_w.shape, lambda i: (0, 0, 0)),
                  pl.BlockSpec((CHUNK, width), fixed2)],
        out_specs=pl.BlockSpec((tm, width), lambda i: (i, 0)),
        out_shape=jax.ShapeDtypeStruct((t, width), BF16),
        compiler_params=_params(("parallel",)),
        name="sgu",
    )(z, z, z, sgu_g.reshape(1, width), sgu_b.reshape(1, width), sgu_w, bias_full)


CONV_PAD = 32
CONV_ROWS = 128
CONV_FILL = 2048
CONV_SLABS = 2


def _conv_kernel(dv_ref, dg_ref, w_ref, cb_ref, o_ref, dpad_ref):
    s_len = dv_ref.shape[0]
    for c in range(CONV_SLABS):
        dpad_ref[c, 0:CONV_PAD, :] = jnp.zeros((CONV_PAD, LANES), F32)

    def fill(i, carry):
        r = pl.multiple_of(i * CONV_FILL, CONV_FILL)
        rows = pl.ds(r, CONV_FILL)
        d = dv_ref[rows, :].astype(F32) * jax.nn.sigmoid(dg_ref[rows, :].astype(F32))
        for c in range(CONV_SLABS):
            dpad_ref[c, pl.ds(CONV_PAD + r, CONV_FILL), :] = d[:, c * LANES:(c + 1) * LANES]
        return carry

    lax.fori_loop(0, s_len // CONV_FILL, fill, 0)

    lead = CONV_PAD - (CONV_K - 1)
    for c in range(CONV_SLABS):
        cs = slice(c * LANES, (c + 1) * LANES)
        taps = [w_ref[k:k + 1, cs] for k in range(CONV_K)]
        for r in range(0, s_len, CONV_ROWS):
            acc = dpad_ref[c, r + lead:r + lead + CONV_ROWS, :] * taps[0]
            for k in range(1, CONV_K):
                acc = acc + dpad_ref[c, r + lead + k:r + lead + k + CONV_ROWS, :] * taps[k]
            o_ref[r:r + CONV_ROWS, cs] = acc + cb_ref[:, cs]


def _conv_taps(z3, col, conv_w, conv_b):
    b, s, cols = z3.shape
    ch = conv_w.shape[1]
    width = CONV_SLABS * LANES
    n_steps = ch // width
    blk = col // width
    vec = lambda a: a.reshape(1, ch)
    return pl.pallas_call(
        _conv_kernel,
        grid=(b, n_steps),
        in_specs=[pl.BlockSpec((None, s, width), lambda i, c: (i, 0, blk + c)),
                  pl.BlockSpec((None, s, width), lambda i, c: (i, 0, blk + n_steps + c)),
                  pl.BlockSpec((CONV_K, width), lambda i, c: (0, c)),
                  pl.BlockSpec((1, width), lambda i, c: (0, c))],
        out_specs=pl.BlockSpec((None, s, width), lambda i, c: (i, 0, c)),
        out_shape=jax.ShapeDtypeStruct((b, s, ch), F32),
        scratch_shapes=[pltpu.VMEM((CONV_SLABS, CONV_PAD + s, LANES), F32)],
        compiler_params=_params(("parallel", "parallel")),
        name="conv_taps",
    )(z3, z3, conv_w, vec(conv_b))


def kernel(x, e_pre_norm, e_w_in, e_pool_w, e_pool_scale, e_w_out, e_post_norm, o_pre_norm, o_w_in, o_sgu_norm_g, o_sgu_norm_b, o_sgu_w, o_sgu_b, o_conv_w, o_conv_b, o_conv_norm_g, o_conv_norm_b, o_w_out, o_post_norm):
    b, s, d = x.shape
    t = b * s
    n_even, n_odd = e_w_in.shape[0], o_w_in.shape[0]
    depth = n_even + n_odd
    pool_width = len(POOL_WINDOWS) * POOL_CH
    n_heads = (e_w_out.shape[1] - pool_width) // HEAD_DIM
    sgu_width = SGU_GROUPS * SGU_CH

    def pre_gain(i):
        return e_pre_norm[i // 2] if i % 2 == 0 else o_pre_norm[i // 2]

    xf = x.reshape(t, d)
    h = _rmsnorm(xf, pre_gain(0))
    for i in range(depth):
        j = i // 2
        g_next = pre_gain(i + 1) if i + 1 < depth else None
        if i % 2 == 0:
            z = _in_proj(h, e_w_in, j)
            z3 = z.reshape(b, s, z.shape[1])
            ya = _pool_mixer(z3, e_pool_w[j], e_pool_scale[j])
            yb = _dilated_attention(z3, 2 * pool_width, n_heads)
            w_out, g_post = e_w_out, e_post_norm[j]
        else:
            z = _in_proj(h, o_w_in, j)
            z3 = z.reshape(b, s, z.shape[1])
            ya = _sgu(z, o_sgu_norm_g[j], o_sgu_norm_b[j], o_sgu_w[j], o_sgu_b[j])
            conv_width = o_conv_w.shape[2]
            conv = _conv_taps(z3, 3 * sgu_width, o_conv_w[j], o_conv_b[j])
            gate_blk = (3 * sgu_width + 2 * conv_width) // conv_width
            yb = (conv.reshape(t, conv_width), z, gate_blk, o_conv_norm_g[j], o_conv_norm_b[j])
            w_out, g_post = o_w_out, o_post_norm[j]
        if not isinstance(yb, tuple):
            yb = yb.reshape(t, -1)
        outs = _out_proj(ya.reshape(t, -1), yb, w_out, j, xf, g_post, g_next)
        if g_next is None:
            xf = outs[0]
        else:
            xf, h = outs
    return xf.reshape(b, s, d)
```

```python
import functools

import jax
import jax.numpy as jnp
import numpy as np
from jax import lax
from jax.experimental import pallas as pl
from jax.experimental.pallas import tpu as pltpu

F32 = jnp.float32
BF16 = jnp.bfloat16

EPS = 1e-6
NEG = -1e30
HEAD_DIM = 128
ROT_DIM = HEAD_DIM // 4
ROT_HALF = ROT_DIM // 2
ROPE_THETA = 500000.0
DILATIONS = (1, 4, 16)
SPAN = 128
ATT_GROUPS = len(DILATIONS)
POOL_WINDOWS = (2, 4, 8, 16)
POOL_CH = 256
SGU_GROUPS = 4
SGU_CH = 256
CHUNK = 128
CONV_K = 31

LANES = 128
VMEM_LIMIT = 56 * 1024 * 1024


def _params(sem, vmem=VMEM_LIMIT):
    return pltpu.CompilerParams(dimension_semantics=sem, vmem_limit_bytes=vmem)


def _silu(x):
    return x * jax.nn.sigmoid(x)


def _rmsnorm_kernel(x_ref, g_ref, o_ref):
    x = x_ref[...]
    ms = jnp.mean(x * x, axis=-1, keepdims=True)
    o_ref[...] = (x * lax.rsqrt(ms + EPS) * g_ref[...]).astype(o_ref.dtype)


def _rmsnorm(x, g, tm=1024):
    t, d = x.shape
    return pl.pallas_call(
        _rmsnorm_kernel,
        grid=(t // tm,),
        in_specs=[pl.BlockSpec((tm, d), lambda i: (i, 0)),
                  pl.BlockSpec((1, d), lambda i: (0, 0))],
        out_specs=pl.BlockSpec((tm, d), lambda i: (i, 0)),
        out_shape=jax.ShapeDtypeStruct((t, d), BF16),
        compiler_params=_params(("parallel",)),
        name="rmsnorm",
    )(x, g.reshape(1, d))


CAST_ROWS = 256


def _cast_weight(w_ref, wb_ref):
    def chunk(i, carry):
        rows = pl.ds(pl.multiple_of(i * CAST_ROWS, CAST_ROWS), CAST_ROWS)
        wb_ref[rows, :] = w_ref[rows, :].astype(wb_ref.dtype)
        return carry

    lax.fori_loop(0, w_ref.shape[0] // CAST_ROWS, chunk, 0)


def _in_proj_kernel(a_ref, w_ref, o_ref, wb_ref):
    first = pl.program_id(1) == 0

    @pl.when(first)
    def _():
        acc = None
        for c in range(0, w_ref.shape[0], CAST_ROWS):
            wc = w_ref[c:c + CAST_ROWS, :].astype(BF16)
            wb_ref[c:c + CAST_ROWS, :] = wc
            part = jnp.dot(a_ref[:, c:c + CAST_ROWS], wc, preferred_element_type=F32)
            acc = part if acc is None else acc + part
        o_ref[...] = acc.astype(o_ref.dtype)

    @pl.when(jnp.logical_not(first))
    def _():
        o_ref[...] = jnp.dot(a_ref[...], wb_ref[...],
                             preferred_element_type=F32).astype(o_ref.dtype)


def _in_proj(a, w_all, layer, tm=2048, tn=1024):
    m, k = a.shape
    n = w_all.shape[2]
    return pl.pallas_call(
        _in_proj_kernel,
        grid=(n // tn, m // tm),
        in_specs=[pl.BlockSpec((tm, k), lambda j, i: (i, 0)),
                  pl.BlockSpec((None, k, tn), lambda j, i: (layer, 0, j))],
        out_specs=pl.BlockSpec((tm, tn), lambda j, i: (i, j)),
        out_shape=jax.ShapeDtypeStruct((m, n), BF16),
        scratch_shapes=[pltpu.VMEM((k, tn), BF16)],
        compiler_params=_params(("parallel", "arbitrary")),
        name="in_proj",
    )(a, w_all)


POOL_ROWS = 128
assert max(POOL_WINDOWS) - 1 <= POOL_ROWS


def _pool_kernel(z_ref, pw_ref, ps_ref, o_ref, pwb_ref, band_ref):
    s_len = z_ref.shape[0]
    width = len(POOL_WINDOWS) * POOL_CH
    ti = lax.broadcasted_iota(jnp.int32, (POOL_ROWS, 2 * POOL_ROWS), 0)
    kj = lax.broadcasted_iota(jnp.int32, (POOL_ROWS, 2 * POOL_ROWS), 1)
    dist = ti + POOL_ROWS - kj
    for g, w in enumerate(POOL_WINDOWS):
        pwb_ref[g] = pw_ref[g].astype(BF16)
        band_ref[g] = jnp.where((dist >= 0) & (dist < w), 1.0, 0.0).astype(BF16)

    for r in range(0, s_len, POOL_ROWS):
        for g, w in enumerate(POOL_WINDOWS):
            cs = slice(g * POOL_CH, (g + 1) * POOL_CH)
            xb = z_ref[r:r + POOL_ROWS, cs]
            x = xb.astype(F32)
            if r == 0:
                acc = jnp.dot(band_ref[g, :, POOL_ROWS:2 * POOL_ROWS], xb, preferred_element_type=F32)
            else:
                acc = jnp.dot(band_ref[g], z_ref[r - POOL_ROWS:r + POOL_ROWS, cs],
                              preferred_element_type=F32)
            if r + 1 >= w:
                cnt = float(w)
            else:
                row = r + lax.broadcasted_iota(jnp.int32, (POOL_ROWS, POOL_CH), 0)
                cnt = jnp.minimum(row + 1, w).astype(F32)
            pooled = acc / cnt - x
            mixed = jnp.dot(pooled.astype(BF16), pwb_ref[g], preferred_element_type=F32)
            gate = z_ref[r:r + POOL_ROWS, width + g * POOL_CH:width + (g + 1) * POOL_CH].astype(F32)
            y = mixed * ps_ref[:, cs] * _silu(gate)
            o_ref[r:r + POOL_ROWS, cs] = y.astype(o_ref.dtype)


def _pool_mixer(z3, pool_w, pool_scale):
    b, s, _ = z3.shape
    width = len(POOL_WINDOWS) * POOL_CH
    return pl.pallas_call(
        _pool_kernel,
        grid=(b,),
        in_specs=[pl.BlockSpec((None, s, 2 * width), lambda i: (i, 0, 0)),
                  pl.BlockSpec(pool_w.shape, lambda i: (0, 0, 0)),
                  pl.BlockSpec((1, width), lambda i: (0, 0))],
        out_specs=pl.BlockSpec((None, s, width), lambda i: (i, 0, 0)),
        out_shape=jax.ShapeDtypeStruct((b, s, width), BF16),
        scratch_shapes=[pltpu.VMEM(pool_w.shape, BF16),
                        pltpu.VMEM((len(POOL_WINDOWS), POOL_ROWS, 2 * POOL_ROWS), BF16)],
        compiler_params=_params(("parallel",)),
        name="pool_mixer",
    )(z3, pool_w, pool_scale.reshape(1, width))


ATT_SCALE_LOG2 = HEAD_DIM ** -0.5 * float(np.log2(np.e))
ATT_ROWS = 1024
ATT_PREP_ROWS = 2048
ATT_MAX_BATCH = 16


def _for_chunks(n, body):
    if n == 1:
        body(0, 0)
    else:
        lax.fori_loop(0, n, body, 0)


def _aligned(start, align):
    return start if isinstance(start, int) else pl.multiple_of(start, align)


def _attn_kernel(q0, k0, v0, q1, k1, v1, q2, k2, v2, gate_ref, cos_ref, sin_ref,
                 o_ref, qkd_ref, qkv_ref, bias_ref, s_ref, ms_ref, acc_ref, m_ref, l_ref):
    s_len = o_ref.shape[0]
    qkv_in = ((q0, k0, v0), (q1, k1, v1), (q2, k2, v2))

    src = lax.broadcasted_iota(jnp.int32, (HEAD_DIM, HEAD_DIM), 0)
    dst = lax.broadcasted_iota(jnp.int32, (HEAD_DIM, HEAD_DIM), 1)
    perm = (jnp.where((dst < ROT_HALF) & (src == dst + ROT_HALF), -1.0, 0.0)
            + jnp.where((dst >= ROT_HALF) & (dst < ROT_DIM) & (src == dst - ROT_HALF), 1.0, 0.0)
            ).astype(BF16)

    strided = [g for g, d in enumerate(DILATIONS) if d > 1]
    dense = [g for g, d in enumerate(DILATIONS) if d == 1]

    prep_rows, merge_rows = min(ATT_PREP_ROWS, s_len), min(ATT_ROWS, s_len)

    def prep(i, c):
        chunk = pl.ds(_aligned(i * prep_rows, prep_rows), prep_rows)
        cos, sin = cos_ref[chunk, :], sin_ref[chunk, :]
        for g, d in enumerate(DILATIONS):
            for j in range(2):
                t = qkv_in[g][j][chunk, :]
                partner = jnp.dot(t, perm, preferred_element_type=F32)
                roped = t.astype(F32) * cos + partner * sin
                if d == 1:
                    qkd_ref[2 * dense.index(g) + j, chunk, :] = roped.astype(BF16)
                else:
                    qkv_ref[3 * strided.index(g) + j, chunk, :] = roped
            if d > 1:
                qkv_ref[3 * strided.index(g) + 2, chunk, :] = qkv_in[g][2][chunk, :].astype(F32)
        return c

    _for_chunks(s_len // prep_rows, prep)

    qi = lax.broadcasted_iota(jnp.int32, (SPAN, 2 * SPAN), 0)
    kj = lax.broadcasted_iota(jnp.int32, (SPAN, 2 * SPAN), 1)
    band, causal = 0, 1
    bias_ref[band] = jnp.where((kj >= qi) & (kj <= qi + SPAN), 0.0, NEG)
    bias_ref[causal] = jnp.where(kj <= qi, 0.0, NEG)

    def rows(d, start, cnt):
        return pl.ds(_aligned(start, SPAN), cnt) if d == 1 else pl.ds(start, cnt, stride=d)

    def operand(g, d, j, start, cnt):
        if d == 1:
            src = qkv_in[g][2] if j == 2 else qkd_ref.at[2 * dense.index(g) + j]
            return src[rows(d, start, cnt), :]
        return qkv_ref[3 * strided.index(g) + j, rows(d, start, cnt), :].astype(BF16)

    def run_batch(blocks, n_keys, mask):
        assert len(blocks) <= ATT_MAX_BATCH

        def blk(j):
            return slice(j * SPAN, (j + 1) * SPAN)

        for j, (g, d, q_start, k_start) in enumerate(blocks):
            s = lax.dot_general(operand(g, d, 0, q_start, SPAN), operand(g, d, 1, k_start, n_keys),
                                (((1,), (1,)), ((), ())), preferred_element_type=F32)
            s = s * ATT_SCALE_LOG2 + bias_ref[mask, :, 0:n_keys]
            s_ref[blk(j), 0:n_keys] = s
            m = jnp.max(s, axis=-1, keepdims=True)
            ms_ref[blk(j), :] = jnp.broadcast_to(m, (SPAN, HEAD_DIM))

        for j, (g, d, q_start, k_start) in enumerate(blocks):
            m = ms_ref[blk(j), :]
            den, ps = None, []
            for c in range(n_keys // HEAD_DIM):
                p = jnp.exp2(s_ref[blk(j), c * HEAD_DIM:(c + 1) * HEAD_DIM] - m)
                ps.append(p.astype(BF16))
                den = p if den is None else den + p
            den = jnp.sum(den, axis=-1, keepdims=True)
            m_ref[g, rows(d, q_start, SPAN), :] = m
            l_ref[g, rows(d, q_start, SPAN), :] = jnp.broadcast_to(den, (SPAN, HEAD_DIM))
            pb = ps[0] if len(ps) == 1 else jnp.concatenate(ps, axis=1)
            acc_ref[g, rows(d, q_start, SPAN), :] = jnp.dot(
                pb, operand(g, d, 2, k_start, n_keys), preferred_element_type=F32)

    first_blocks = [(g, d, r, r) for g, d in enumerate(DILATIONS) for r in range(d)]
    for lo in range(0, len(first_blocks), ATT_MAX_BATCH):
        run_batch(first_blocks[lo:lo + ATT_MAX_BATCH], SPAN, causal)
    for g, d in enumerate(DILATIONS):
        stride_blk = d * SPAN
        band_blocks = [(g, d, r + i * stride_blk, r + (i - 1) * stride_blk)
                       for i in range(1, s_len // stride_blk) for r in range(d)]
        for lo in range(0, len(band_blocks), ATT_MAX_BATCH):
            run_batch(band_blocks[lo:lo + ATT_MAX_BATCH], 2 * SPAN, band)

    def merge(i, c):
        rows = pl.ds(_aligned(i * merge_rows, merge_rows), merge_rows)
        ms = [m_ref[g, rows, :] for g in range(ATT_GROUPS)]
        m = functools.reduce(jnp.maximum, ms)
        num = den = None
        for g in range(ATT_GROUPS):
            e = jnp.exp2(ms[g] - m)
            ng, dg = e * acc_ref[g, rows, :], e * l_ref[g, rows, :]
            num, den = (ng, dg) if num is None else (num + ng, den + dg)
        gate = gate_ref[rows, :].astype(F32)
        o_ref[rows, :] = (num / den * _silu(gate)).astype(o_ref.dtype)
        return c

    _for_chunks(s_len // merge_rows, merge)


def _rope_tables(s_len):
    pos = jnp.arange(s_len, dtype=F32)
    inv_freq = jnp.power(ROPE_THETA, -jnp.arange(0, ROT_DIM, 2, dtype=F32) / ROT_DIM)
    ang = pos[:, None] * inv_freq[None, :]
    cos, sin = jnp.cos(ang), jnp.sin(ang)
    rest = jnp.zeros((s_len, HEAD_DIM - ROT_DIM), F32)
    return (jnp.concatenate([cos, cos, rest + 1.0], axis=-1),
            jnp.concatenate([sin, sin, rest], axis=-1))


def _dilated_attention(z3, q_col, n_heads):
    b, s, _ = z3.shape
    per = ATT_GROUPS * n_heads
    qb = q_col // HEAD_DIM
    cos_t, sin_t = _rope_tables(s)

    def head_spec(blk0):
        return pl.BlockSpec((None, s, HEAD_DIM), lambda i, h, blk0=blk0: (i, 0, blk0 + h))

    in_specs = []
    for g in range(ATT_GROUPS):
        for j in range(3):
            in_specs.append(head_spec(qb + j * per + g * n_heads))
    in_specs.append(head_spec(qb + 3 * per))
    tab_spec = pl.BlockSpec((s, HEAD_DIM), lambda i, h: (0, 0))
    in_specs += [tab_spec, tab_spec]
    return pl.pallas_call(
        _attn_kernel,
        grid=(b, n_heads),
        in_specs=in_specs,
        out_specs=pl.BlockSpec((None, s, HEAD_DIM), lambda i, h: (i, 0, h)),
        out_shape=jax.ShapeDtypeStruct((b, s, n_heads * HEAD_DIM), BF16),
        scratch_shapes=[pltpu.VMEM((2 * sum(d == 1 for d in DILATIONS), s, HEAD_DIM), BF16),
                        pltpu.VMEM((3 * sum(d > 1 for d in DILATIONS), s, HEAD_DIM), F32),
                        pltpu.VMEM((2, SPAN, 2 * SPAN), F32),
                        pltpu.VMEM((ATT_MAX_BATCH * SPAN, 2 * SPAN), F32),
                        pltpu.VMEM((ATT_MAX_BATCH * SPAN, HEAD_DIM), F32),
                        pltpu.VMEM((ATT_GROUPS, s, HEAD_DIM), F32),
                        pltpu.VMEM((ATT_GROUPS, s, HEAD_DIM), F32),
                        pltpu.VMEM((ATT_GROUPS, s, HEAD_DIM), F32)],
        compiler_params=_params(("parallel", "parallel")),
        name="dilated_attention",
    )(*([z3] * 10), cos_t, sin_t)


OUT_SUB_ROWS = 512


def _out_proj_kernel(*refs, with_next, conv_in):
    refs = list(refs)
    ya_ref = refs.pop(0)
    if conv_in:
        conv_ref, gate_ref, ng_ref, nb_ref = refs[:4]
        del refs[:4]
    else:
        yb_ref = refs.pop(0)
    w_ref, x_ref, gpost_ref = refs[:3]
    del refs[:3]
    if with_next:
        gnext_ref, xo_ref, ho_ref, wb_ref = refs
    else:
        xo_ref, wb_ref = refs

    @pl.when(pl.program_id(0) == 0)
    def _():
        _cast_weight(w_ref, wb_ref)

    half = ya_ref.shape[1]
    for r in range(0, ya_ref.shape[0], OUT_SUB_ROWS):
        rows = slice(r, r + OUT_SUB_ROWS)
        if conv_in:
            c = conv_ref[rows, :]
            mu = jnp.mean(c, axis=-1, keepdims=True)
            cc = c - mu
            var = jnp.mean(cc * cc, axis=-1, keepdims=True)
            yb = _silu(cc * lax.rsqrt(var + EPS) * ng_ref[...] + nb_ref[...])
            yb = (yb * _silu(gate_ref[rows, :].astype(F32))).astype(BF16)
        else:
            yb = yb_ref[rows, :]
        y = jnp.dot(ya_ref[rows, :], wb_ref[0:half, :], preferred_element_type=F32)
        y = y + jnp.dot(yb, wb_ref[half:2 * half, :], preferred_element_type=F32)
        yn = y * lax.rsqrt(jnp.mean(y * y, axis=-1, keepdims=True) + EPS) * gpost_ref[...]
        xn = x_ref[rows, :] + yn
        xo_ref[rows, :] = xn
        if with_next:
            hn = xn * lax.rsqrt(jnp.mean(xn * xn, axis=-1, keepdims=True) + EPS) * gnext_ref[...]
            ho_ref[rows, :] = hn.astype(ho_ref.dtype)


def _out_proj(ya, yb, w_all, layer, x, g_post, g_next=None, tm=512):
    t, half = ya.shape
    _, k, d = w_all.shape
    with_next = g_next is not None
    conv_in = isinstance(yb, tuple)
    row = lambda i: (i, 0)
    fixed = lambda i: (0, 0)
    in_specs = [pl.BlockSpec((tm, half), row)]
    args = [ya]
    if conv_in:
        conv, z, gate_blk, cn_g, cn_b = yb
        in_specs += [pl.BlockSpec((tm, half), row),
                     pl.BlockSpec((tm, half), lambda i: (i, gate_blk)),
                     pl.BlockSpec((1, half), fixed), pl.BlockSpec((1, half), fixed)]
        args += [conv, z, cn_g.reshape(1, half), cn_b.reshape(1, half)]
    else:
        in_specs.append(pl.BlockSpec((tm, half), row))
        args.append(yb)
    in_specs += [pl.BlockSpec((None, k, d), lambda i: (layer, 0, 0), pipeline_mode=pl.Buffered(1)),
                 pl.BlockSpec((tm, d), row),
                 pl.BlockSpec((1, d), fixed)]
    args += [w_all, x, g_post.reshape(1, d)]
    out_specs = [pl.BlockSpec((tm, d), row)]
    out_shape = [jax.ShapeDtypeStruct((t, d), F32)]
    if with_next:
        in_specs.append(pl.BlockSpec((1, d), fixed))
        args.append(g_next.reshape(1, d))
        out_specs.append(pl.BlockSpec((tm, d), row))
        out_shape.append(jax.ShapeDtypeStruct((t, d), BF16))
    return pl.pallas_call(
        functools.partial(_out_proj_kernel, with_next=with_next, conv_in=conv_in),
        grid=(t // tm,),
        in_specs=in_specs,
        out_specs=out_specs,
        out_shape=out_shape,
        scratch_shapes=[pltpu.VMEM((k, d), BF16)],
        compiler_params=_params(("arbitrary",)),
        name="out_proj",
    )(*args)


def _sgu_kernel(u_ref, v_ref, cg_ref, g_ref, b_ref, ws_ref, bias_ref, o_ref):
    tm = u_ref.shape[0]
    ii = lax.broadcasted_iota(jnp.int32, (CHUNK, CHUNK), 0)
    jj = lax.broadcasted_iota(jnp.int32, (CHUNK, CHUNK), 1)
    tril = jj <= ii
    for c in range(tm // CHUNK):
        rows = slice(c * CHUNK, (c + 1) * CHUNK)
        v = v_ref[rows, :].astype(F32)
        mu = jnp.mean(v, axis=-1, keepdims=True)
        vc = v - mu
        var = jnp.mean(vc * vc, axis=-1, keepdims=True)
        vn = (vc * lax.rsqrt(var + EPS) * g_ref[...] + b_ref[...]).astype(BF16)
        for h in range(SGU_GROUPS):
            cs = slice(h * SGU_CH, (h + 1) * SGU_CH)
            wm = jnp.where(tril, ws_ref[h], 0.0).astype(BF16)
            sg = jnp.dot(wm, vn[:, cs], preferred_element_type=F32) + bias_ref[:, cs]
            y = u_ref[rows, cs].astype(F32) * sg * _silu(cg_ref[rows, cs].astype(F32))
            o_ref[rows, cs] = y.astype(o_ref.dtype)


def _sgu(z, sgu_g, sgu_b, sgu_w, sgu_bias, tm=1024):
    t = z.shape[0]
    width = SGU_GROUPS * SGU_CH
    bias_full = jnp.repeat(sgu_bias.T, SGU_CH, axis=1)
    fixed2 = lambda i: (0, 0)
    return pl.pallas_call(
        _sgu_kernel,
        grid=(t // tm,),
        in_specs=[pl.BlockSpec((tm, width), lambda i: (i, 0)),
                  pl.BlockSpec((tm, width), lambda i: (i, 1)),
                  pl.BlockSpec((tm, width), lambda i: (i, 2)),
                  pl.BlockSpec((1, width), fixed2), pl.BlockSpec((1, width), fixed2),
                  pl.BlockSpec(sgu_w.shape, lambda i: (0, 0, 0)),
                  pl.BlockSpec((CHUNK, width), fixed2)],
        out_specs=pl.BlockSpec((tm, width), lambda i: (i, 0)),
        out_shape=jax.ShapeDtypeStruct((t, width), BF16),
        compiler_params=_params(("parallel",)),
        name="sgu",
    )(z, z, z, sgu_g.reshape(1, width), sgu_b.reshape(1, width), sgu_w, bias_full)


CONV_PAD = 32
CONV_ROWS = 128
CONV_FILL = 2048
CONV_SLABS = 2


def _conv_kernel(dv_ref, dg_ref, w_ref, cb_ref, o_ref, dpad_ref):
    s_len = dv_ref.shape[0]
    for c in range(CONV_SLABS):
        dpad_ref[c, 0:CONV_PAD, :] = jnp.zeros((CONV_PAD, LANES), F32)

    def fill(i, carry):
        r = pl.multiple_of(i * CONV_FILL, CONV_FILL)
        rows = pl.ds(r, CONV_FILL)
        d = dv_ref[rows, :].astype(F32) * jax.nn.sigmoid(dg_ref[rows, :].astype(F32))
        for c in range(CONV_SLABS):
            dpad_ref[c, pl.ds(CONV_PAD + r, CONV_FILL), :] = d[:, c * LANES:(c + 1) * LANES]
        return carry

    lax.fori_loop(0, s_len // CONV_FILL, fill, 0)

    lead = CONV_PAD - (CONV_K - 1)
    for c in range(CONV_SLABS):
        cs = slice(c * LANES, (c + 1) * LANES)
        taps = [w_ref[k:k + 1, cs] for k in range(CONV_K)]
        for r in range(0, s_len, CONV_ROWS):
            acc = dpad_ref[c, r + lead:r + lead + CONV_ROWS, :] * taps[0]
            for k in range(1, CONV_K):
                acc = acc + dpad_ref[c, r + lead + k:r + lead + k + CONV_ROWS, :] * taps[k]
            o_ref[r:r + CONV_ROWS, cs] = acc + cb_ref[:, cs]


def _conv_taps(z3, col, conv_w, conv_b):
    b, s, cols = z3.shape
    ch = conv_w.shape[1]
    width = CONV_SLABS * LANES
    n_steps = ch // width
    blk = col // width
    vec = lambda a: a.reshape(1, ch)
    return pl.pallas_call(
        _conv_kernel,
        grid=(b, n_steps),
        in_specs=[pl.BlockSpec((None, s, width), lambda i, c: (i, 0, blk + c)),
                  pl.BlockSpec((None, s, width), lambda i, c: (i, 0, blk + n_steps + c)),
                  pl.BlockSpec((CONV_K, width), lambda i, c: (0, c)),
                  pl.BlockSpec((1, width), lambda i, c: (0, c))],
        out_specs=pl.BlockSpec((None, s, width), lambda i, c: (i, 0, c)),
        out_shape=jax.ShapeDtypeStruct((b, s, ch), F32),
        scratch_shapes=[pltpu.VMEM((CONV_SLABS, CONV_PAD + s, LANES), F32)],
        compiler_params=_params(("parallel", "parallel")),
        name="conv_taps",
    )(z3, z3, conv_w, vec(conv_b))


def kernel(x, e_pre_norm, e_w_in, e_pool_w, e_pool_scale, e_w_out, e_post_norm, o_pre_norm, o_w_in, o_sgu_norm_g, o_sgu_norm_b, o_sgu_w, o_sgu_b, o_conv_w, o_conv_b, o_conv_norm_g, o_conv_norm_b, o_w_out, o_post_norm):
    b, s, d = x.shape
    t = b * s
    n_even, n_odd = e_w_in.shape[0], o_w_in.shape[0]
    depth = n_even + n_odd
    pool_width = len(POOL_WINDOWS) * POOL_CH
    n_heads = (e_w_out.shape[1] - pool_width) // HEAD_DIM
    sgu_width = SGU_GROUPS * SGU_CH

    def pre_gain(i):
        return e_pre_norm[i // 2] if i % 2 == 0 else o_pre_norm[i // 2]

    xf = x.reshape(t, d)
    h = _rmsnorm(xf, pre_gain(0))
    for i in range(depth):
        j = i // 2
        g_next = pre_gain(i + 1) if i + 1 < depth else None
        if i % 2 == 0:
            z = _in_proj(h, e_w_in, j)
            z3 = z.reshape(b, s, z.shape[1])
            ya = _pool_mixer(z3, e_pool_w[j], e_pool_scale[j])
            yb = _dilated_attention(z3, 2 * pool_width, n_heads)
            w_out, g_post = e_w_out, e_post_norm[j]
        else:
            z = _in_proj(h, o_w_in, j)
            z3 = z.reshape(b, s, z.shape[1])
            ya = _sgu(z, o_sgu_norm_g[j], o_sgu_norm_b[j], o_sgu_w[j], o_sgu_b[j])
            conv_width = o_conv_w.shape[2]
            conv = _conv_taps(z3, 3 * sgu_width, o_conv_w[j], o_conv_b[j])
            gate_blk = (3 * sgu_width + 2 * conv_width) // conv_width
            yb = (conv.reshape(t, conv_width), z, gate_blk, o_conv_norm_g[j], o_conv_norm_b[j])
            w_out, g_post = o_w_out, o_post_norm[j]
        if not isinstance(yb, tuple):
            yb = yb.reshape(t, -1)
        outs = _out_proj(ya.reshape(t, -1), yb, w_out, j, xf, g_post, g_next)
        if g_next is None:
            xf = outs[0]
        else:
            xf, h = outs
    return xf.reshape(b, s, d)
```
